```python
import jax, jax.numpy as jnp
from jax import lax
import numpy as np

D_MODEL = 2048
BATCH = 4
SEQ = 2048
DEPTH = 2
DEC_BATCH = 128
DEC_SEQ = 1
PAST_LEN = 16384
PAGE_SIZE = 128

D_MIX = D_MODEL
D_ML = D_MIX // 2
D_CM = D_MIX - D_ML
ML_HEADS = 4
ML_HD = D_ML // ML_HEADS
CM_GROUPS = 4
CM_GD = D_CM // CM_GROUPS
CM_CHUNK = 128
ML_CHUNK = 128
D_IN = 5 * D_ML + 2 * ML_HEADS + 3 * D_CM
EPS = 1e-6

kernel_name = 'hymba_mlstm_chunkmlp_decoder_step'


def rmsnorm(x, g):
    xf = x.astype(jnp.float32)
    y = xf * lax.rsqrt(jnp.mean(xf * xf, axis=-1, keepdims=True) + EPS)
    return (y * g.astype(jnp.float32)).astype(x.dtype)


def layernorm(x, g):
    xf = x.astype(jnp.float32)
    mu = jnp.mean(xf, axis=-1, keepdims=True)
    xc = xf - mu
    y = xc * lax.rsqrt(jnp.mean(xc * xc, axis=-1, keepdims=True) + EPS)
    return (y * g.astype(jnp.float32)).astype(x.dtype)


def mlstm_chunk(carry, inp):
    C, n, m = carry
    q, k, v, ig, lf = inp
    L = q.shape[2]
    b = jnp.cumsum(lf, axis=-1)
    causal = jnp.tril(jnp.ones((L, L), dtype=bool))
    D = jnp.where(causal, b[..., :, None] - b[..., None, :] + ig[..., None, :], -jnp.inf)
    a = b + m[..., None]
    mt = jnp.maximum(a, jnp.max(D, axis=-1))
    w_intra = jnp.exp(D - mt[..., None])
    w_inter = jnp.exp(a - mt)
    s = jnp.einsum('bhtd,bhsd->bhts', q, k) * w_intra
    num = w_inter[..., None] * jnp.einsum('bhvd,bhtd->bhtv', C, q) + jnp.einsum('bhts,bhsv->bhtv', s, v)
    den = w_inter * jnp.einsum('bhd,bhtd->bht', n, q) + jnp.sum(s, axis=-1)
    h = num / jnp.maximum(jnp.abs(den), jnp.exp(-mt))[..., None]
    m_new = mt[..., -1]
    g_end = jnp.exp(b[..., -1:] - b + ig - m_new[..., None])
    decay = jnp.exp(a[..., -1] - m_new)
    C_new = decay[..., None, None] * C + jnp.einsum('bhs,bhsv,bhsd->bhvd', g_end, v, k)
    n_new = decay[..., None] * n + jnp.einsum('bhs,bhsd->bhd', g_end, k)
    return (C_new, n_new, m_new), h


def mlstm(q, k, v, ig, lf, C0, n0, m0):
    B, T, H, d = q.shape
    L = min(ML_CHUNK, T)
    nc = T // L
    c4 = lambda t: t.reshape(B, nc, L, H, t.shape[-1]).transpose(1, 0, 3, 2, 4)
    c3 = lambda t: t.reshape(B, nc, L, H).transpose(1, 0, 3, 2)
    (C1, n1, m1), hs = lax.scan(mlstm_chunk, (C0, n0, m0), (c4(q), c4(k), c4(v), c3(ig), c3(lf)))
    h = hs.transpose(1, 0, 3, 2, 4).reshape(B, T, H, hs.shape[-1])
    return h, C1, n1, m1


def spatial_mix(v, w_s, b_s):
    B, T, G, dg = v.shape
    L = min(CM_CHUNK, T)
    nc = T // L
    w = jnp.where(jnp.tril(jnp.ones((L, L), dtype=bool)), w_s[:, :L, :L], 0)
    vc = v.reshape(B, nc, L, G, dg)
    out = jnp.einsum('gts,bcsgd->bctgd', w, vc) + b_s[:, :L].T[None, None, :, :, None]
    return out.reshape(B, T, G, dg)


def mixer_layer(x, c, C0, n0, m0, g_norm, w_ada, b_ada, w_in, b_igate, b_fgate, g_mh, g_cmv, w_s, b_s, w_out):
    B, T, _ = x.shape
    f32 = jnp.float32
    shift, scale, gate = jnp.split(jax.nn.silu(c) @ w_ada + b_ada, 3, axis=-1)
    h = rmsnorm(x, g_norm) * (1 + scale[:, None, :]) + shift[:, None, :]
    p = h @ w_in
    sizes = [D_ML] * 5 + [ML_HEADS] * 2 + [D_CM] * 3
    offs = [int(o) for o in np.cumsum(sizes)[:-1]]
    q, k, v, o, z_ml, i_pre, f_pre, u, v_cm, z_cm = jnp.split(p, offs, axis=-1)
    heads = lambda t: t.reshape(B, T, ML_HEADS, ML_HD)
    ig = (i_pre + b_igate).astype(f32)
    lf = jax.nn.log_sigmoid((f_pre + b_fgate).astype(f32))
    h_ml, C1, n1, m1 = mlstm(heads(q).astype(f32), heads(k).astype(f32) * (ML_HD ** -0.5),
                             heads(v).astype(f32), ig, lf,
                             C0.astype(f32), n0.astype(f32), m0.astype(f32))
    h_ml = jax.nn.sigmoid(heads(o)) * h_ml.astype(x.dtype)
    h_ml = rmsnorm(h_ml, g_mh.reshape(ML_HEADS, ML_HD)).reshape(B, T, D_ML) * jax.nn.silu(z_ml)
    u = jax.nn.gelu(u)
    v_n = layernorm(jax.nn.gelu(v_cm).reshape(B, T, CM_GROUPS, CM_GD), g_cmv.reshape(CM_GROUPS, CM_GD))
    h_cm = u * spatial_mix(v_n, w_s, b_s).reshape(B, T, D_CM) * jax.nn.silu(z_cm)
    out = jnp.concatenate([h_ml, h_cm], axis=-1) @ w_out
    x = x + gate[:, None, :] * out
    return x, C1, n1, m1, v_n.reshape(B, T, D_CM)


def setup_inputs(seed: int = 0) -> dict:
    key = jax.random.key(seed)
    ks = jax.random.split(key, 24)
    nrm = jax.random.normal
    f32 = jnp.float32
    b_f = jnp.linspace(3.0, 6.0, ML_HEADS, dtype=f32)[None, :] + 0.1 * nrm(ks[12], (DEPTH, ML_HEADS), f32)
    return {
        'x_prompt': nrm(ks[0], (BATCH, SEQ, D_MODEL), f32),
        'x_sample': nrm(ks[1], (DEC_BATCH, DEC_SEQ, D_MODEL), f32),
        'state_C': 0.02 * nrm(ks[2], (DEPTH, DEC_BATCH, ML_HEADS, ML_HD, ML_HD), f32),
        'state_n': 0.02 * nrm(ks[3], (DEPTH, DEC_BATCH, ML_HEADS, ML_HD), f32),
        'state_m': jax.random.uniform(ks[4], (DEPTH, DEC_BATCH, ML_HEADS), f32, 0.0, 2.0),
        'c_prompt': nrm(ks[5], (BATCH, D_MODEL), f32),
        'c_sample': nrm(ks[6], (DEC_BATCH, D_MODEL), f32),
        'g_norm': 1.0 + 0.02 * nrm(ks[7], (DEPTH, D_MODEL), f32),
        'w_ada': 0.5 * D_MODEL ** -0.5 * nrm(ks[8], (DEPTH, D_MODEL, 3 * D_MODEL), f32),
        'b_ada': 0.02 * nrm(ks[9], (DEPTH, 3 * D_MODEL), f32),
        'w_in': D_MODEL ** -0.5 * nrm(ks[10], (DEPTH, D_MODEL, D_IN), f32),
        'b_igate': 0.1 * nrm(ks[11], (DEPTH, ML_HEADS), f32),
        'b_fgate': b_f,
        'g_mh': 1.0 + 0.02 * nrm(ks[13], (DEPTH, D_ML), f32),
        'g_cmv': 1.0 + 0.02 * nrm(ks[14], (DEPTH, D_CM), f32),
        'w_s': CM_CHUNK ** -0.5 * nrm(ks[15], (DEPTH, CM_GROUPS, CM_CHUNK, CM_CHUNK), f32),
        'b_s': 1.0 + 0.1 * nrm(ks[16], (DEPTH, CM_GROUPS, CM_CHUNK), f32),
        'w_out': D_MIX ** -0.5 * nrm(ks[17], (DEPTH, D_MIX, D_MODEL), f32),
        'g_final': 1.0 + 0.02 * nrm(ks[18], (D_MODEL,), f32),
    }


def reference(x_prompt, x_sample, state_C, state_n, state_m, c_prompt, c_sample,
              g_norm, w_ada, b_ada, w_in, b_igate, b_fgate, g_mh, g_cmv, w_s, b_s, w_out, g_final):
    f32 = jnp.float32
    zC = jnp.zeros((BATCH, ML_HEADS, ML_HD, ML_HD), f32)
    zn = jnp.zeros((BATCH, ML_HEADS, ML_HD), f32)
    zm = jnp.zeros((BATCH, ML_HEADS), f32)
    yp, ys = x_prompt, x_sample
    Cp_l, np_l, mp_l, Cs_l, ns_l, ms_l, vs_l = [], [], [], [], [], [], []
    for l in range(DEPTH):
        lp = (g_norm[l], w_ada[l], b_ada[l], w_in[l], b_igate[l], b_fgate[l],
              g_mh[l], g_cmv[l], w_s[l], b_s[l], w_out[l])
        yp, Cp, np_, mp, _ = mixer_layer(yp, c_prompt, zC, zn, zm, *lp)
        ys, Cs, ns, ms, vs = mixer_layer(ys, c_sample, state_C[l], state_n[l], state_m[l], *lp)
        Cp_l.append(Cp); np_l.append(np_); mp_l.append(mp)
        Cs_l.append(Cs); ns_l.append(ns); ms_l.append(ms); vs_l.append(vs)
    y_prompt = rmsnorm(yp, g_final)
    y_sample = rmsnorm(ys, g_final)
    return (y_prompt, y_sample,
            jnp.stack(Cp_l), jnp.stack(np_l), jnp.stack(mp_l),
            jnp.stack(Cs_l), jnp.stack(ns_l), jnp.stack(ms_l), jnp.stack(vs_l))
```

```python
import functools

import jax
import jax.numpy as jnp
from jax import lax
from jax.experimental import pallas as pl
from jax.experimental.pallas import tpu as pltpu

f32 = jnp.float32
bf16 = jnp.bfloat16

D_MODEL = 2048
D_ML = 1024
D_CM = 1024
HEADS = 4
HD = 256
GROUPS = 4
CHUNK = 128
N_MAIN = 5 * D_ML + 3 * D_CM
GATE_OFF = 5 * D_ML
GATE_PAD = 128
EPS = 1e-6
OFF_Q, OFF_K, OFF_V, OFF_O, OFF_ZML, OFF_U, OFF_VCM, OFF_ZCM = (i * 1024 for i in range(8))

VMEM_LIMIT = 56 * 1024 * 1024


def _cparams(sem):
    return pltpu.CompilerParams(dimension_semantics=sem, vmem_limit_bytes=VMEM_LIMIT)


def _sigmoid(x):
    return 1.0 / (1.0 + jnp.exp(-x))


def _silu(x):
    return x * _sigmoid(x)


def _gelu(x):
    return 0.5 * x * (1.0 + jnp.tanh(0.7978845608028654 * (x + 0.044715 * (x * x * x))))


def _log_sigmoid(x):
    return jnp.minimum(x, 0.0) - jnp.log1p(jnp.exp(-jnp.abs(x)))


def _rms(x, g):
    return x * lax.rsqrt(jnp.mean(x * x, axis=-1, keepdims=True) + EPS) * g


def _layernorm(x, g):
    mu = jnp.mean(x, axis=-1, keepdims=True)
    xc = x - mu
    return xc * lax.rsqrt(jnp.mean(xc * xc, axis=-1, keepdims=True) + EPS) * g


def _ada_kernel(c_ref, w_ref, b_ref, o_ref):
    a = _silu(c_ref[...]).astype(bf16)
    o_ref[...] = jnp.dot(a, w_ref[...].astype(bf16), preferred_element_type=f32) + b_ref[...]


def _ada(c_all, w_ada, b_ada):
    depth, _, n3 = w_ada.shape
    rows = c_all.shape[0]
    tn = 512
    return pl.pallas_call(
        _ada_kernel,
        grid=(depth, n3 // tn),
        in_specs=[
            pl.BlockSpec((rows, D_MODEL), lambda l, j: (0, 0)),
            pl.BlockSpec((None, D_MODEL, tn), lambda l, j: (l, 0, j)),
            pl.BlockSpec((None, 1, tn), lambda l, j: (l, 0, j)),
        ],
        out_specs=pl.BlockSpec((None, rows, tn), lambda l, j: (l, 0, j)),
        out_shape=jax.ShapeDtypeStruct((depth, rows, n3), f32),
        compiler_params=_cparams(("arbitrary", "arbitrary")),
        name="ada",
    )(c_all, w_ada, b_ada.reshape(depth, 1, n3))


def _in_kernel(x_ref, sc_ref, sh_ref, gn_ref, w_ref, wg_ref, p_ref, g_ref, h_s, *, tm):
    @pl.when(pl.program_id(1) == 0)
    def _():
        rc = min(tm, 128)

        def body(r, carry):
            rows = pl.ds(pl.multiple_of(r * rc, rc), rc)
            sc = sc_ref[...] if sc_ref.shape[0] == 1 else sc_ref[rows, :]
            sh = sh_ref[...] if sh_ref.shape[0] == 1 else sh_ref[rows, :]
            h = _rms(x_ref[rows, :], gn_ref[...]) * (1.0 + sc) + sh
            h_s[rows, :] = h.astype(bf16)
            return carry

        lax.fori_loop(0, tm // rc, body, 0)
        g_ref[...] = jnp.dot(h_s[...], wg_ref[...], preferred_element_type=f32)

    p_ref[...] = jnp.dot(h_s[...], w_ref[...], preferred_element_type=f32)


def _in_proj(x2, sc, sh, gn, wb, wg, *, tm, tn, per_row_mod):
    m = x2.shape[0]
    rows_per_b = m // sc.shape[0] if not per_row_mod else None
    if per_row_mod:
        mod_spec = pl.BlockSpec((tm, D_MODEL), lambda i, j: (i, 0))
    else:
        mod_spec = pl.BlockSpec((None, 1, D_MODEL), lambda i, j: ((i * tm) // rows_per_b, 0, 0))
    return pl.pallas_call(
        functools.partial(_in_kernel, tm=tm),
        grid=(m // tm, N_MAIN // tn),
        in_specs=[
            pl.BlockSpec((tm, D_MODEL), lambda i, j: (i, 0)),
            mod_spec,
            mod_spec,
            pl.BlockSpec((1, D_MODEL), lambda i, j: (0, 0)),
            pl.BlockSpec((D_MODEL, tn), lambda i, j: (0, j)),
            pl.BlockSpec((D_MODEL, GATE_PAD), lambda i, j: (0, 0)),
        ],
        out_specs=[
            pl.BlockSpec((tm, tn), lambda i, j: (i, j)),
            pl.BlockSpec((tm, GATE_PAD), lambda i, j: (i, 0)),
        ],
        out_shape=[
            jax.ShapeDtypeStruct((m, N_MAIN), f32),
            jax.ShapeDtypeStruct((m, GATE_PAD), f32),
        ],
        scratch_shapes=[pltpu.VMEM((tm, D_MODEL), bf16)],
        compiler_params=_cparams(("arbitrary", "arbitrary")),
        name="in_proj",
    )(x2, sc, sh, gn, wb, wg)


def _mix_kernel(p_ref, g_ref, x_ref, gate_ref, wout_ref, ws_ref, bst_ref, gmh_ref, gcmv_ref,
                bias_ref, gfin_ref, xo_ref, c_ref, n_ref, m_ref, merged_s, *, cpb, final):
    @pl.when(pl.program_id(1) == 0)
    def _():
        c_ref[...] = jnp.zeros_like(c_ref)
        n_ref[...] = jnp.zeros_like(n_ref)
        m_ref[...] = jnp.zeros_like(m_ref)

    row = lax.broadcasted_iota(jnp.int32, (CHUNK, CHUNK), 0)
    col = lax.broadcasted_iota(jnp.int32, (CHUNK, CHUNK), 1)
    causal = col <= row
    tril1 = causal.astype(f32)

    for ci in range(cpb):
        rows = slice(ci * CHUNK, (ci + 1) * CHUNK)
        gt = g_ref[rows, :] + bias_ref[...]
        lf = _log_sigmoid(gt)
        cum = jnp.dot(tril1, lf, precision=lax.Precision.HIGHEST, preferred_element_type=f32)
        gt_t = gt.T
        cum_t = cum.T
        for h in range(HEADS):
            hs = slice(h * HD, (h + 1) * HD)
            q = p_ref[rows, OFF_Q + h * HD:OFF_Q + (h + 1) * HD]
            k = p_ref[rows, OFF_K + h * HD:OFF_K + (h + 1) * HD] * (HD ** -0.5)
            v = p_ref[rows, OFF_V + h * HD:OFF_V + (h + 1) * HD]
            qb, kb, vb = q.astype(bf16), k.astype(bf16), v.astype(bf16)
            ig_col = gt[:, h:h + 1]
            ig_row = gt_t[h:h + 1, :]
            b_col = cum[:, HEADS + h:HEADS + h + 1]
            b_row = cum_t[HEADS + h:HEADS + h + 1, :]
            m_old = m_ref[h:h + 1, 0:1]
            c_old = c_ref[h]
            n_old = n_ref[h:h + 1, :]

            dmat = jnp.where(causal, b_col - b_row + ig_row, -jnp.inf)
            a_col = b_col + m_old
            mt = jnp.maximum(a_col, jnp.max(dmat, axis=-1, keepdims=True))
            w_intra = jnp.exp(dmat - mt)
            w_inter = jnp.exp(a_col - mt)
            s = lax.dot_general(qb, kb, (((1,), (1,)), ((), ())), preferred_element_type=f32) * w_intra
            cq = lax.dot_general(qb, c_old.astype(bf16), (((1,), (1,)), ((), ())),
                                 preferred_element_type=f32)
            num = w_inter * cq + jnp.dot(s.astype(bf16), vb, preferred_element_type=f32)
            den = (w_inter * jnp.sum(q * n_old, axis=-1, keepdims=True)
                   + jnp.sum(s, axis=-1, keepdims=True))
            hml = num / jnp.maximum(jnp.abs(den), jnp.exp(-mt))

            m_new = mt[CHUNK - 1:CHUNK, :]
            b_last = b_col[CHUNK - 1:CHUNK, :]
            g_end = jnp.exp(b_last - b_col + ig_col - m_new)
            decay = jnp.exp(b_last + m_old - m_new)
            gv_t = (g_end * v).T.astype(bf16)
            c_ref[h] = decay * c_old + jnp.dot(gv_t, kb, preferred_element_type=f32)
            n_ref[h:h + 1, :] = decay * n_old + jnp.sum(g_end * k, axis=0, keepdims=True)
            m_ref[h:h + 1, :] = jnp.broadcast_to(m_new, (1, m_ref.shape[1]))

            o = p_ref[rows, OFF_O + h * HD:OFF_O + (h + 1) * HD]
            z = p_ref[rows, OFF_ZML + h * HD:OFF_ZML + (h + 1) * HD]
            y = _rms(_sigmoid(o) * hml, gmh_ref[:, hs]) * _silu(z)
            merged_s[rows, hs] = y.astype(bf16)

        for g in range(GROUPS):
            gs = slice(g * HD, (g + 1) * HD)
            u = p_ref[rows, OFF_U + g * HD:OFF_U + (g + 1) * HD]
            vc = p_ref[rows, OFF_VCM + g * HD:OFF_VCM + (g + 1) * HD]
            z = p_ref[rows, OFF_ZCM + g * HD:OFF_ZCM + (g + 1) * HD]
            vn = _layernorm(_gelu(vc), gcmv_ref[:, gs])
            w = jnp.where(causal, ws_ref[g], 0.0).astype(bf16)
            sp = jnp.dot(w, vn.astype(bf16), preferred_element_type=f32) + bst_ref[:, g:g + 1]
            merged_s[rows, D_ML + g * HD:D_ML + (g + 1) * HD] = (_gelu(u) * sp * _silu(z)).astype(bf16)

    out = jnp.dot(merged_s[...], wout_ref[...], preferred_element_type=f32)
    xn = x_ref[...] + gate_ref[...] * out
    if final:
        xn = _rms(xn, gfin_ref[...])
    xo_ref[...] = xn


def _mix(p, g, x2, gate, wout, ws, bst, gmh, gcmv, bias, gfin, *, batch, cpb, final):
    m = x2.shape[0]
    ts = cpb * CHUNK
    steps = m // batch // ts
    tok = lambda b, c: (b * steps + c, 0)
    full2 = lambda b, c: (0, 0)
    return pl.pallas_call(
        functools.partial(_mix_kernel, cpb=cpb, final=final),
        grid=(batch, steps),
        in_specs=[
            pl.BlockSpec((ts, N_MAIN), tok),
            pl.BlockSpec((ts, GATE_PAD), tok),
            pl.BlockSpec((ts, D_MODEL), tok),
            pl.BlockSpec((None, 1, D_MODEL), lambda b, c: (b, 0, 0)),
            pl.BlockSpec((D_MODEL, D_MODEL), full2),
            pl.BlockSpec((GROUPS, CHUNK, CHUNK), lambda b, c: (0, 0, 0)),
            pl.BlockSpec((CHUNK, GATE_PAD), full2),
            pl.BlockSpec((1, D_ML), full2),
            pl.BlockSpec((1, D_CM), full2),
            pl.BlockSpec((1, GATE_PAD), full2),
            pl.BlockSpec((1, D_MODEL), full2),
        ],
        out_specs=[
            pl.BlockSpec((ts, D_MODEL), tok),
            pl.BlockSpec((None, HEADS, HD, HD), lambda b, c: (b, 0, 0, 0)),
            pl.BlockSpec((None, HEADS, HD), lambda b, c: (b, 0, 0)),
            pl.BlockSpec((None, 8, 128), lambda b, c: (b, 0, 0)),
        ],
        out_shape=[
            jax.ShapeDtypeStruct((m, D_MODEL), f32),
            jax.ShapeDtypeStruct((batch, HEADS, HD, HD), f32),
            jax.ShapeDtypeStruct((batch, HEADS, HD), f32),
            jax.ShapeDtypeStruct((batch, 8, 128), f32),
        ],
        scratch_shapes=[pltpu.VMEM((ts, D_MODEL), bf16)],
        compiler_params=_cparams(("arbitrary", "arbitrary")),
        name="mix",
    )(p, g, x2, gate, wout, ws, bst, gmh, gcmv, bias, gfin)


def _spre_kernel(p_ref, g_ref, bias_ref, n_ref, m_ref,
                 gk_ref, nn_ref, mn_ref, dec_ref, s_ref, stab_ref, vt_ref):
    gt = g_ref[...] + bias_ref[...]
    lf = _log_sigmoid(gt)
    for h in range(HEADS):
        hs = slice(h * HD, (h + 1) * HD)
        q = p_ref[:, OFF_Q + h * HD:OFF_Q + (h + 1) * HD]
        k = p_ref[:, OFF_K + h * HD:OFF_K + (h + 1) * HD] * (HD ** -0.5)
        ig = gt[:, h:h + 1]
        a = lf[:, HEADS + h:HEADS + h + 1] + m_ref[:, h:h + 1]
        mt = jnp.maximum(a, ig)
        w_intra = jnp.exp(ig - mt)
        w_inter = jnp.exp(a - mt)
        n_old = n_ref[:, hs]
        s = jnp.sum(q * k, axis=-1, keepdims=True) * w_intra
        den = w_inter * jnp.sum(n_old * q, axis=-1, keepdims=True) + s
        gk = w_intra * k
        gk_ref[:, hs] = gk
        nn_ref[:, hs] = w_inter * n_old + gk
        mn_ref[:, h:h + 1] = mt
        dec_ref[:, h:h + 1] = w_inter
        s_ref[:, h:h + 1] = s
        stab_ref[:, h:h + 1] = jnp.maximum(jnp.abs(den), jnp.exp(-mt))
    for t in range(D_ML // 128):
        vt_ref[t * 128:(t + 1) * 128, :] = p_ref[:, OFF_V + t * 128:OFF_V + (t + 1) * 128].T


def _spre(p, g, bias, n_old, m_old):
    b = p.shape[0]
    small = jax.ShapeDtypeStruct((b, HEADS), f32)
    wide = jax.ShapeDtypeStruct((b, D_ML), f32)
    return pl.pallas_call(
        _spre_kernel,
        grid=(1,),
        in_specs=[
            pl.BlockSpec((b, 3 * D_ML), lambda i: (0, 0)),
            pl.BlockSpec((b, GATE_PAD), lambda i: (0, 0)),
            pl.BlockSpec((1, GATE_PAD), lambda i: (0, 0)),
            pl.BlockSpec((b, D_ML), lambda i: (0, 0)),
            pl.BlockSpec((b, HEADS), lambda i: (0, 0)),
        ],
        out_specs=[
            pl.BlockSpec((b, D_ML), lambda i: (0, 0)),
            pl.BlockSpec((b, D_ML), lambda i: (0, 0)),
            pl.BlockSpec((b, HEADS), lambda i: (0, 0)),
            pl.BlockSpec((b, HEADS), lambda i: (0, 0)),
            pl.BlockSpec((b, HEADS), lambda i: (0, 0)),
            pl.BlockSpec((b, HEADS), lambda i: (0, 0)),
            pl.BlockSpec((D_ML, b), lambda i: (0, 0)),
        ],
        out_shape=[wide, wide, small, small, small, small, jax.ShapeDtypeStruct((D_ML, b), f32)],
        compiler_params=_cparams(("arbitrary",)),
        name="sample_pre",
    )(p, g, bias, n_old, m_old)


def _sc_kernel(dec_ref, c_ref, q_ref, gk_ref, vt_ref, *rest, bb):
    co_ref, cqt_ref = rest[-2:]
    j = pl.program_id(0)

    @pl.when(j == 0)
    def _():
        cqt_ref[...] = jnp.zeros_like(cqt_ref)

    lane = lax.broadcasted_iota(jnp.int32, (HD, cqt_ref.shape[1]), 1)
    for i in range(bb):
        b = j * bb + i
        onehot = (lane == b).astype(f32)
        for h in range(HEADS):
            hs = slice(h * HD, (h + 1) * HD)
            c_old = c_ref[i, h]
            vcol = jnp.sum(vt_ref[hs, :] * onehot, axis=-1, keepdims=True)
            cq = jnp.sum(c_old * q_ref[i:i + 1, hs], axis=-1, keepdims=True)
            cqt_ref[hs, :] += cq * onehot
            co_ref[i, h] = dec_ref[b * HEADS + h] * c_old + vcol * gk_ref[i:i + 1, hs]


def _sc(dec_flat, state_c, layer, p, gk, vt, c_out_prev, *, bb):
    b = p.shape[0]
    in_specs = [
        pl.BlockSpec(memory_space=pltpu.SMEM),
        pl.BlockSpec((None, bb, HEADS, HD, HD), lambda j: (layer, j, 0, 0, 0)),
        pl.BlockSpec((bb, D_ML), lambda j: (j, 0)),
        pl.BlockSpec((bb, D_ML), lambda j: (j, 0)),
        pl.BlockSpec((D_ML, b), lambda j: (0, 0)),
    ]
    args = [dec_flat, state_c, p, gk, vt]
    aliases = {}
    if c_out_prev is not None:
        in_specs.append(pl.BlockSpec(memory_space=pl.ANY))
        args.append(c_out_prev)
        aliases = {5: 0}
    return pl.pallas_call(
        functools.partial(_sc_kernel, bb=bb),
        grid=(b // bb,),
        in_specs=in_specs,
        out_specs=[
            pl.BlockSpec((None, bb, HEADS, HD, HD), lambda j: (layer, j, 0, 0, 0)),
            pl.BlockSpec((D_ML, b), lambda j: (0, 0)),
        ],
        out_shape=[
            jax.ShapeDtypeStruct(state_c.shape, f32),
            jax.ShapeDtypeStruct((D_ML, b), f32),
        ],
        input_output_aliases=aliases,
        compiler_params=_cparams(("arbitrary",)),
        name="sample_state",
    )(*args)


def _spost_kernel(ws0_ref, bs0_ref, cqt_ref, p_ref, dec_ref, s_ref, stab_ref, x_ref, gate_ref, wout_ref,
                  gmh_ref, gcmv_ref, gfin_ref, xo_ref, vn_ref, merged_s, *, final):
    for h in range(HEADS):
        hs = slice(h * HD, (h + 1) * HD)
        cq = jnp.concatenate([cqt_ref[h * HD + t * 128:h * HD + (t + 1) * 128, :].T
                              for t in range(HD // 128)], axis=1)
        v = p_ref[:, OFF_V + h * HD:OFF_V + (h + 1) * HD]
        o = p_ref[:, OFF_O + h * HD:OFF_O + (h + 1) * HD]
        z = p_ref[:, OFF_ZML + h * HD:OFF_ZML + (h + 1) * HD]
        num = dec_ref[:, h:h + 1] * cq + s_ref[:, h:h + 1] * v
        hml = num / stab_ref[:, h:h + 1]
        y = _rms(_sigmoid(o) * hml, gmh_ref[:, hs]) * _silu(z)
        merged_s[:, hs] = y.astype(bf16)
    for g in range(GROUPS):
        gs = slice(g * HD, (g + 1) * HD)
        u = p_ref[:, OFF_U + g * HD:OFF_U + (g + 1) * HD]
        vc = p_ref[:, OFF_VCM + g * HD:OFF_VCM + (g + 1) * HD]
        z = p_ref[:, OFF_ZCM + g * HD:OFF_ZCM + (g + 1) * HD]
        vn = _layernorm(_gelu(vc), gcmv_ref[:, gs])
        vn_ref[:, gs] = vn
        sp = ws0_ref[g] * vn + bs0_ref[g]
        merged_s[:, D_ML + g * HD:D_ML + (g + 1) * HD] = (_gelu(u) * sp * _silu(z)).astype(bf16)
    out = jnp.dot(merged_s[...], wout_ref[...], preferred_element_type=f32)
    xn = x_ref[...] + gate_ref[...] * out
    if final:
        xn = _rms(xn, gfin_ref[...])
    xo_ref[...] = xn


def _spost(ws0, bs0, cqt, p, dec, s, stab, x2, gate, wout, gmh, gcmv, gfin, *, final):
    b = p.shape[0]
    full = lambda i: (0, 0)
    smem = pl.BlockSpec(memory_space=pltpu.SMEM)
    small = pl.BlockSpec((b, HEADS), full)
    return pl.pallas_call(
        functools.partial(_spost_kernel, final=final),
        grid=(1,),
        in_specs=[
            smem, smem,
            pl.BlockSpec((D_ML, b), full),
            pl.BlockSpec((b, N_MAIN), full),
            small, small, small,
            pl.BlockSpec((b, D_MODEL), full),
            pl.BlockSpec((b, D_MODEL), full),
            pl.BlockSpec((D_MODEL, D_MODEL), full),
            pl.BlockSpec((1, D_ML), full),
            pl.BlockSpec((1, D_CM), full),
            pl.BlockSpec((1, D_MODEL), full),
        ],
        out_specs=[pl.BlockSpec((b, D_MODEL), full), pl.BlockSpec((b, D_CM), full)],
        out_shape=[jax.ShapeDtypeStruct((b, D_MODEL), f32), jax.ShapeDtypeStruct((b, D_CM), f32)],
        scratch_shapes=[pltpu.VMEM((b, D_MODEL), bf16)],
        compiler_params=_cparams(("arbitrary",)),
        name="sample_post",
    )(ws0, bs0, cqt, p, dec, s, stab, x2, gate, wout, gmh, gcmv, gfin)


def kernel(x_prompt, x_sample, state_C, state_n, state_m, c_prompt, c_sample, g_norm, w_ada, b_ada, w_in,
           b_igate, b_fgate, g_mh, g_cmv, w_s, b_s, w_out, g_final):
    depth = w_in.shape[0]
    bp, seq, _ = x_prompt.shape
    bs = x_sample.shape[0]

    c_all = jnp.concatenate([c_prompt, c_sample, jnp.zeros((4, D_MODEL), f32)], axis=0)
    mod = _ada(c_all, w_ada, b_ada)

    xp = x_prompt.reshape(bp * seq, D_MODEL)
    xs = x_sample.reshape(bs, D_MODEL)
    gfin = g_final.reshape(1, D_MODEL)
    outs = {k: [] for k in ("cp", "np", "mp", "ns", "ms", "vs")}
    cs_all = None

    for l in range(depth):
        wl = w_in[l]
        wb = jnp.concatenate([wl[:, :GATE_OFF], wl[:, GATE_OFF + 2 * HEADS:]], axis=1).astype(bf16)
        wg = jnp.pad(wl[:, GATE_OFF:GATE_OFF + 2 * HEADS], ((0, 0), (0, GATE_PAD - 2 * HEADS))).astype(bf16)
        wout = w_out[l].astype(bf16)
        gn = g_norm[l].reshape(1, D_MODEL)
        gmh = g_mh[l].reshape(1, D_ML)
        gcmv = g_cmv[l].reshape(1, D_CM)
        bias = jnp.pad(jnp.concatenate([b_igate[l], b_fgate[l]]), (0, GATE_PAD - 2 * HEADS)).reshape(1, GATE_PAD)
        bst = jnp.pad(b_s[l].T, ((0, 0), (0, GATE_PAD - GROUPS)))
        final = l == depth - 1

        shift_p, scale_p, gate_p = (mod[l, :bp, i * D_MODEL:(i + 1) * D_MODEL].reshape(bp, 1, D_MODEL)
                                    for i in range(3))
        shift_s, scale_s, gate_s = (mod[l, bp:bp + bs, i * D_MODEL:(i + 1) * D_MODEL] for i in range(3))

        p, g = _in_proj(xp, scale_p, shift_p, gn, wb, wg, tm=1024, tn=1024, per_row_mod=False)
        xp, c_new, n_new, m_new = _mix(p, g, xp, gate_p, wout, w_s[l], bst, gmh, gcmv, bias, gfin,
                                       batch=bp, cpb=2, final=final)
        outs["cp"].append(c_new)
        outs["np"].append(n_new)
        outs["mp"].append(m_new[:, :HEADS, 0])

        p, g = _in_proj(xs, scale_s, shift_s, gn, wb, wg, tm=bs, tn=1024, per_row_mod=True)
        gk, nn, mn, dec, s, stab, vt = _spre(p, g, bias, state_n[l].reshape(bs, D_ML), state_m[l])
        cs_all, cqt = _sc(dec.reshape(bs * HEADS), state_C, l, p, gk, vt, cs_all, bb=8)
        xs, vn = _spost(w_s[l, :, 0, 0], b_s[l, :, 0], cqt, p, dec, s, stab, xs, gate_s, wout,
                        gmh, gcmv, gfin, final=final)
        outs["ns"].append(nn.reshape(bs, HEADS, HD))
        outs["ms"].append(mn)
        outs["vs"].append(vn.reshape(bs, 1, D_CM))

    return (xp.reshape(bp, seq, D_MODEL), xs.reshape(bs, 1, D_MODEL),
            jnp.stack(outs["cp"]), jnp.stack(outs["np"]), jnp.stack(outs["mp"]),
            cs_all, jnp.stack(outs["ns"]), jnp.stack(outs["ms"]), jnp.stack(outs["vs"]))
```

```python
import functools

import jax
import jax.numpy as jnp
from jax import lax
from jax.experimental import pallas as pl
from jax.experimental.pallas import tpu as pltpu

f32 = jnp.float32
bf16 = jnp.bfloat16

D_MODEL = 2048
D_ML = 1024
D_CM = 1024
HEADS = 4
HD = 256
GROUPS = 4
CHUNK = 128
LANES = 128
KIND_W = 1024
N_KINDS = 8
N_MAIN = N_KINDS * KIND_W
N_GATES = 2 * HEADS
GATE_KIND = 5
EPS = 1e-6
OFF_Q, OFF_K, OFF_V, OFF_O, OFF_ZML, OFF_U, OFF_VN, OFF_ZCM = (i * KIND_W for i in range(N_KINDS))
MOD_SAMPLE_ROW = 0
VMEM_LIMIT = 56 * 1024 * 1024


def _cparams(sem):
    return pltpu.CompilerParams(dimension_semantics=sem, vmem_limit_bytes=VMEM_LIMIT)


def _sigmoid(x):
    return 1.0 / (1.0 + jnp.exp(-x))


def _silu(x):
    return x * _sigmoid(x)


def _gelu(x):
    return 0.5 * x * (1.0 + jnp.tanh(0.7978845608028654 * (x + 0.044715 * (x * x * x))))


def _log_sigmoid(x):
    return jnp.minimum(x, 0.0) - jnp.log1p(jnp.exp(-jnp.abs(x)))


def _rms(x, g):
    return x * lax.rsqrt(jnp.mean(x * x, axis=-1, keepdims=True) + EPS) * g


def _layernorm(x, g):
    mu = jnp.mean(x, axis=-1, keepdims=True)
    xc = x - mu
    return xc * lax.rsqrt(jnp.mean(xc * xc, axis=-1, keepdims=True) + EPS) * g


def _gate_bias_row(bi_ref, bf_ref, layer):
    lane = lax.broadcasted_iota(jnp.int32, (1, LANES), 1)
    row = jnp.zeros((1, LANES), f32)
    for h in range(HEADS):
        row = jnp.where(lane == h, bi_ref[layer, h], row)
        row = jnp.where(lane == HEADS + h, bf_ref[layer, h], row)
    return row


def _modulated_norm(x, gn, sc, sh):
    return _rms(x, gn) * (1.0 + sc) + sh


def _ada_kernel(c_ref, w_ref, b_ref, o_ref):
    a = _silu(c_ref[...]).astype(bf16)
    o_ref[...] = jnp.dot(a, w_ref[...].astype(bf16), preferred_element_type=f32) + b_ref[...]


def _ada(c_all, w_ada, b_ada):
    depth, _, n3 = w_ada.shape
    rows = c_all.shape[0]
    tn = 512
    per = D_MODEL // tn
    return pl.pallas_call(
        _ada_kernel,
        grid=(depth, n3 // tn),
        in_specs=[
            pl.BlockSpec((rows, D_MODEL), lambda l, j: (0, 0)),
            pl.BlockSpec((None, D_MODEL, tn), lambda l, j: (l, 0, j)),
            pl.BlockSpec((None, 1, tn), lambda l, j: (l, 0, j)),
        ],
        out_specs=pl.BlockSpec((None, None, rows, tn), lambda l, j: (l, j // per, 0, j % per)),
        out_shape=jax.ShapeDtypeStruct((depth, 3, rows, D_MODEL), f32),
        compiler_params=_cparams(("arbitrary", "arbitrary")),
        name="ada",
    )(c_all, w_ada, b_ada.reshape(depth, 1, n3))


def _cast_kernel(w_ref, o_ref):
    o_ref[...] = w_ref[...].astype(bf16)


def _cast_bf16(w):
    depth, r, c = w.shape
    tr = 512
    return pl.pallas_call(
        _cast_kernel,
        grid=(depth, r // tr),
        in_specs=[pl.BlockSpec((None, tr, c), lambda l, i: (l, i, 0))],
        out_specs=pl.BlockSpec((None, tr, c), lambda l, i: (l, i, 0)),
        out_shape=jax.ShapeDtypeStruct(w.shape, bf16),
        compiler_params=_cparams(("arbitrary", "arbitrary")),
        name="cast_w_out",
    )(w)


def _prenorm_kernel(x_ref, sh_ref, sc_ref, gn_ref, wg_ref, h_ref, g_ref, *, layer, per_row_mod, rows_per_b):
    if per_row_mod:
        sh, sc = sh_ref[...], sc_ref[...]
    else:
        b = (pl.program_id(0) * x_ref.shape[0]) // rows_per_b
        sh, sc = sh_ref[pl.ds(b, 1), :], sc_ref[pl.ds(b, 1), :]
    h = _modulated_norm(x_ref[...], gn_ref[layer:layer + 1, :], sc, sh).astype(bf16)
    h_ref[...] = h
    g_ref[...] = jnp.dot(h, wg_ref[...].astype(bf16), preferred_element_type=f32)


def _prenorm(x2, mod, g_norm, w_in, *, layer, tm, mod_row_block, mod_rows, per_row_mod, rows_per_b):
    m = x2.shape[0]
    mod_spec = lambda k: pl.BlockSpec((None, None, mod_rows, D_MODEL), lambda i: (layer, k, mod_row_block, 0))
    return pl.pallas_call(
        functools.partial(_prenorm_kernel, layer=layer, per_row_mod=per_row_mod, rows_per_b=rows_per_b),
        grid=(m // tm,),
        in_specs=[
            pl.BlockSpec((tm, D_MODEL), lambda i: (i, 0)),
            mod_spec(0),
            mod_spec(1),
            pl.BlockSpec(g_norm.shape, lambda i: (0, 0)),
            pl.BlockSpec((None, D_MODEL, LANES), lambda i: (layer, 0, GATE_KIND * KIND_W // LANES)),
        ],
        out_specs=[pl.BlockSpec((tm, D_MODEL), lambda i: (i, 0)), pl.BlockSpec((tm, LANES), lambda i: (i, 0))],
        out_shape=[jax.ShapeDtypeStruct((m, D_MODEL), bf16), jax.ShapeDtypeStruct((m, LANES), f32)],
        compiler_params=_cparams(("arbitrary",)),
        name="prenorm",
    )(x2, mod, mod, g_norm, w_in)


def _in_kernel(h_ref, wm_ref, wx_ref, gcmv_ref, p_ref, wb_s, *, layer, tm, rc):
    j = pl.program_id(0)
    kc = 256

    @pl.when(pl.program_id(1) == 0)
    def _():
        @pl.when(j < GATE_KIND)
        def _():
            for r in range(D_MODEL // kc):
                wb_s[r * kc:(r + 1) * kc, :] = wm_ref[r * kc:(r + 1) * kc, :].astype(bf16)

        @pl.when(j >= GATE_KIND)
        def _():
            for r in range(D_MODEL // kc):
                wide = jnp.concatenate([wm_ref[r * kc:(r + 1) * kc, :], wx_ref[r * kc:(r + 1) * kc, :]], axis=1)
                wb_s[r * kc:(r + 1) * kc, :] = wide[:, N_GATES:N_GATES + KIND_W].astype(bf16)

    def vn_act(acc):
        g = gcmv_ref[layer:layer + 1, :]
        return jnp.concatenate([_layernorm(_gelu(acc[:, t * HD:(t + 1) * HD]), g[:, t * HD:(t + 1) * HD])
                                for t in range(GROUPS)], axis=1)

    def run(act):
        for r in range(tm // rc):
            acc = jnp.dot(h_ref[r * rc:(r + 1) * rc, :], wb_s[...], preferred_element_type=f32)
            p_ref[r * rc:(r + 1) * rc, :] = act(acc).astype(p_ref.dtype)

    pl.when((j == 0) | (j == 2))(lambda: run(lambda a: a))
    pl.when(j == 1)(lambda: run(lambda a: a * (HD ** -0.5)))
    pl.when(j == 3)(lambda: run(_sigmoid))
    pl.when((j == 4) | (j == 7))(lambda: run(_silu))
    pl.when(j == 5)(lambda: run(_gelu))
    pl.when(j == 6)(lambda: run(vn_act))


def _in_proj(h, w_in, g_cmv, *, layer, tm, out_dtype):
    m = h.shape[0]
    rc = min(tm, 256)
    return pl.pallas_call(
        functools.partial(_in_kernel, layer=layer, tm=tm, rc=rc),
        grid=(N_KINDS, m // tm),
        in_specs=[
            pl.BlockSpec((tm, D_MODEL), lambda j, i: (i, 0)),
            pl.BlockSpec((None, D_MODEL, KIND_W), lambda j, i: (layer, 0, j)),
            pl.BlockSpec((None, D_MODEL, LANES), lambda j, i: (layer, 0, (j + 1) * (KIND_W // LANES))),
            pl.BlockSpec(g_cmv.shape, lambda j, i: (0, 0)),
        ],
        out_specs=pl.BlockSpec((tm, KIND_W), lambda j, i: (i, j)),
        out_shape=jax.ShapeDtypeStruct((m, N_MAIN), out_dtype),
        scratch_shapes=[pltpu.VMEM((D_MODEL, KIND_W), bf16)],
        compiler_params=_cparams(("arbitrary", "arbitrary")),
        name="in_proj",
    )(h, w_in, w_in, g_cmv)


def _mix_kernel(bi_ref, bf_ref, p_ref, g_ref, x_ref, gate_ref, wout_ref, ws_ref, bs_ref, gmh_ref, gfin_ref,
                *rest, layer, cpb, final):
    if final:
        xo_ref, c_ref, n_ref, m_ref, merged_s = rest[-5:]
    else:
        sh_ref, sc_ref, gn_ref, wg_ref = rest[:4]
        xo_ref, hn_ref, gn_out_ref, c_ref, n_ref, m_ref, merged_s = rest[-7:]
    b = pl.program_id(0)

    @pl.when(pl.program_id(1) == 0)
    def _():
        c_ref[...] = jnp.zeros_like(c_ref)
        n_ref[...] = jnp.zeros_like(n_ref)
        m_ref[...] = jnp.zeros_like(m_ref)

    row = lax.broadcasted_iota(jnp.int32, (CHUNK, CHUNK), 0)
    col = lax.broadcasted_iota(jnp.int32, (CHUNK, CHUNK), 1)
    causal = col <= row
    tril1 = causal.astype(f32)
    bias_row = _gate_bias_row(bi_ref, bf_ref, layer)
    nt = (((1,), (1,)), ((), ()))

    for ci in range(cpb):
        rows = slice(ci * CHUNK, (ci + 1) * CHUNK)
        gt = g_ref[rows, :] + bias_row
        lf = _log_sigmoid(gt)
        cum = jnp.dot(tril1, lf, precision=lax.Precision.HIGHEST, preferred_element_type=f32)
        gt_t = gt.T
        cum_t = cum.T
        for h in range(HEADS):
            hs = slice(h * HD, (h + 1) * HD)
            qb = p_ref[rows, OFF_Q + h * HD:OFF_Q + (h + 1) * HD]
            kb = p_ref[rows, OFF_K + h * HD:OFF_K + (h + 1) * HD]
            vb = p_ref[rows, OFF_V + h * HD:OFF_V + (h + 1) * HD]
            ig_col = gt[:, h:h + 1]
            ig_row = gt_t[h:h + 1, :]
            b_col = cum[:, HEADS + h:HEADS + h + 1]
            b_row = cum_t[HEADS + h:HEADS + h + 1, :]
            m_old = m_ref[h:h + 1, 0:1]
            c_old = c_ref[h]
            n_old = n_ref[h:h + 1, :]

            dmat = jnp.where(causal, b_col - b_row + ig_row, -jnp.inf)
            a_col = b_col + m_old
            mt = jnp.maximum(a_col, jnp.max(dmat, axis=-1, keepdims=True))
            w_intra = jnp.exp(dmat - mt)
            w_inter = jnp.exp(a_col - mt)
            s = lax.dot_general(qb, kb, nt, preferred_element_type=f32) * w_intra
            cq = lax.dot_general(qb, c_old.astype(bf16), nt, preferred_element_type=f32)
            num = w_inter * cq + jnp.dot(s.astype(bf16), vb, preferred_element_type=f32)
            den = (w_inter * jnp.sum(qb.astype(f32) * n_old, axis=-1, keepdims=True)
                   + jnp.sum(s, axis=-1, keepdims=True))
            hml = num / jnp.maximum(jnp.abs(den), jnp.exp(-mt))

            m_new = mt[CHUNK - 1:CHUNK, :]
            b_last = b_col[CHUNK - 1:CHUNK, :]
            g_end = jnp.exp(b_last - b_col + ig_col - m_new)
            decay = jnp.exp(b_last + m_old - m_new)
            gv_t = (g_end * vb.astype(f32)).T.astype(bf16)
            c_ref[h] = decay * c_old + jnp.dot(gv_t, kb, preferred_element_type=f32)
            n_ref[h:h + 1, :] = decay * n_old + jnp.sum(g_end * kb.astype(f32), axis=0, keepdims=True)
            m_ref[h:h + 1, :] = jnp.broadcast_to(m_new, (1, m_ref.shape[1]))

            o_act = p_ref[rows, OFF_O + h * HD:OFF_O + (h + 1) * HD].astype(f32)
            z_act = p_ref[rows, OFF_ZML + h * HD:OFF_ZML + (h + 1) * HD].astype(f32)
            y = _rms(o_act * hml, gmh_ref[layer:layer + 1, hs]) * z_act
            merged_s[rows, hs] = y.astype(bf16)

        for g in range(GROUPS):
            u_act = p_ref[rows, OFF_U + g * HD:OFF_U + (g + 1) * HD].astype(f32)
            vn = p_ref[rows, OFF_VN + g * HD:OFF_VN + (g + 1) * HD]
            z_act = p_ref[rows, OFF_ZCM + g * HD:OFF_ZCM + (g + 1) * HD].astype(f32)
            w = jnp.where(causal, ws_ref[g], 0.0).astype(bf16)
            b_col = jnp.sum(jnp.where(row == col, bs_ref[g:g + 1, :], 0.0), axis=-1, keepdims=True)
            sp = jnp.dot(w, vn, preferred_element_type=f32) + b_col
            merged_s[rows, D_ML + g * HD:D_ML + (g + 1) * HD] = (u_act * sp * z_act).astype(bf16)

    out = jnp.dot(merged_s[...], wout_ref[...], preferred_element_type=f32)
    xn = x_ref[...] + gate_ref[pl.ds(b, 1), :] * out
    if final:
        xo_ref[...] = _rms(xn, gfin_ref[...])
    else:
        xo_ref[...] = xn
        hn = _modulated_norm(xn, gn_ref[layer + 1:layer + 2, :], sc_ref[pl.ds(b, 1), :],
                             sh_ref[pl.ds(b, 1), :]).astype(bf16)
        hn_ref[...] = hn
        gn_out_ref[...] = jnp.dot(hn, wg_ref[...].astype(bf16), preferred_element_type=f32)


def _mix(p, g, x2, mod, wout, w_s, b_s, g_mh, gfin, b_ig, b_fg, g_norm, w_in, prev, *,
         layer, batch, mod_row_block, cpb, final):
    m = x2.shape[0]
    depth = w_s.shape[0]
    ts = cpb * CHUNK
    steps = m // batch // ts
    tok = lambda b, c: (b * steps + c, 0)
    full2 = lambda b, c: (0, 0)
    smem = pl.BlockSpec(memory_space=pltpu.SMEM)
    mod_spec = lambda lyr, k: pl.BlockSpec((None, None, 8, D_MODEL), lambda b, c: (lyr, k, mod_row_block, 0))
    in_specs = [
        smem, smem,
        pl.BlockSpec((ts, N_MAIN), tok),
        pl.BlockSpec((ts, LANES), tok),
        pl.BlockSpec((ts, D_MODEL), tok),
        mod_spec(layer, 2),
        pl.BlockSpec((None, D_MODEL, D_MODEL), lambda b, c: (layer, 0, 0)),
        pl.BlockSpec((None, GROUPS, CHUNK, CHUNK), lambda b, c: (layer, 0, 0, 0)),
        pl.BlockSpec((None, GROUPS, CHUNK), lambda b, c: (layer, 0, 0)),
        pl.BlockSpec(g_mh.shape, full2),
        pl.BlockSpec((1, D_MODEL), full2),
    ]
    args = [b_ig, b_fg, p, g, x2, mod, wout, w_s, b_s, g_mh, gfin]
    out_specs = [pl.BlockSpec((ts, D_MODEL), tok)]
    out_shape = [jax.ShapeDtypeStruct((m, D_MODEL), f32)]
    if not final:
        in_specs += [
            mod_spec(layer + 1, 0), mod_spec(layer + 1, 1),
            pl.BlockSpec(g_norm.shape, full2),
            pl.BlockSpec((None, D_MODEL, LANES), lambda b, c: (layer + 1, 0, GATE_KIND * KIND_W // LANES)),
        ]
        args += [mod, mod, g_norm, w_in]
        out_specs += [pl.BlockSpec((ts, D_MODEL), tok), pl.BlockSpec((ts, LANES), tok)]
        out_shape += [jax.ShapeDtypeStruct((m, D_MODEL), bf16), jax.ShapeDtypeStruct((m, LANES), f32)]
    n_state_out = len(out_specs)
    out_specs += [
        pl.BlockSpec((None, None, HEADS, HD, HD), lambda b, c: (layer, b, 0, 0, 0)),
        pl.BlockSpec((None, None, HEADS, HD), lambda b, c: (layer, b, 0, 0)),
        pl.BlockSpec((None, None, 8, LANES), lambda b, c: (layer, b, 0, 0)),
    ]
    out_shape += [
        jax.ShapeDtypeStruct((depth, batch, HEADS, HD, HD), f32),
        jax.ShapeDtypeStruct((depth, batch, HEADS, HD), f32),
        jax.ShapeDtypeStruct((depth, batch, 8, LANES), f32),
    ]
    aliases = {}
    if prev is not None:
        for t, arr in enumerate(prev):
            aliases[len(args)] = n_state_out + t
            in_specs.append(pl.BlockSpec(memory_space=pl.ANY))
            args.append(arr)
    kern = functools.partial(_mix_kernel, layer=layer, cpb=cpb, final=final)
    return pl.pallas_call(
        kern,
        grid=(batch, steps),
        in_specs=in_specs,
        out_specs=out_specs,
        out_shape=out_shape,
        scratch_shapes=[pltpu.VMEM((ts, D_MODEL), bf16)],
        input_output_aliases=aliases,
        compiler_params=_cparams(("arbitrary", "arbitrary")),
        name="mix",
    )(*args)


def _spre_kernel(bi_ref, bf_ref, p_ref, g_ref, n_ref, m_ref,
                 gk_ref, nn_ref, mn_ref, dec_ref, s_ref, stab_ref, vt_ref, *, layer):
    gt = g_ref[...] + _gate_bias_row(bi_ref, bf_ref, layer)
    lf = _log_sigmoid(gt)
    for h in range(HEADS):
        hs = slice(h * HD, (h + 1) * HD)
        q = p_ref[:, OFF_Q + h * HD:OFF_Q + (h + 1) * HD]
        k = p_ref[:, OFF_K + h * HD:OFF_K + (h + 1) * HD]
        ig = gt[:, h:h + 1]
        a = lf[:, HEADS + h:HEADS + h + 1] + m_ref[:, h:h + 1]
        mt = jnp.maximum(a, ig)
        w_intra = jnp.exp(ig - mt)
        w_inter = jnp.exp(a - mt)
        n_old = n_ref[:, hs]
        s = jnp.sum(q * k, axis=-1, keepdims=True) * w_intra
        den = w_inter * jnp.sum(n_old * q, axis=-1, keepdims=True) + s
        gk = w_intra * k
        gk_ref[:, hs] = gk
        nn_ref[:, hs] = w_inter * n_old + gk
        mn_ref[:, h:h + 1] = mt
        dec_ref[:, h:h + 1] = w_inter
        s_ref[:, h:h + 1] = s
        stab_ref[:, h:h + 1] = jnp.maximum(jnp.abs(den), jnp.exp(-mt))
    for t in range(D_ML // LANES):
        vt_ref[t * LANES:(t + 1) * LANES, :] = p_ref[:, OFF_V + t * LANES:OFF_V + (t + 1) * LANES].T


def _spre(p, g, b_ig, b_fg, state_n, state_m, *, layer):
    b = p.shape[0]
    full = lambda i: (0, 0)
    smem = pl.BlockSpec(memory_space=pltpu.SMEM)
    small = jax.ShapeDtypeStruct((b, HEADS), f32)
    wide = jax.ShapeDtypeStruct((b, D_ML), f32)
    small_spec = pl.BlockSpec((b, HEADS), full)
    wide_spec = pl.BlockSpec((b, D_ML), full)
    return pl.pallas_call(
        functools.partial(_spre_kernel, layer=layer),
        grid=(1,),
        in_specs=[
            smem, smem,
            pl.BlockSpec((b, 3 * D_ML), full),
            pl.BlockSpec((b, LANES), full),
            pl.BlockSpec((None, b, D_ML), lambda i: (layer, 0, 0)),
            pl.BlockSpec((None, b, HEADS), lambda i: (layer, 0, 0)),
        ],
        out_specs=[wide_spec, wide_spec, small_spec, small_spec, small_spec, small_spec,
                   pl.BlockSpec((D_ML, b), full)],
        out_shape=[wide, wide, small, small, small, small, jax.ShapeDtypeStruct((D_ML, b), f32)],
        compiler_params=_cparams(("arbitrary",)),
        name="sample_pre",
    )(b_ig, b_fg, p, g, state_n, state_m)


def _sc_kernel(dec_ref, c_ref, q_ref, gk_ref, vt_ref, *rest, bb):
    co_ref, cqt_ref = rest[-2:]
    j = pl.program_id(0)

    @pl.when(j == 0)
    def _():
        cqt_ref[...] = jnp.zeros_like(cqt_ref)

    lane = lax.broadcasted_iota(jnp.int32, (HD, cqt_ref.shape[1]), 1)
    for i in range(bb):
        b = j * bb + i
        onehot = (lane == b).astype(f32)
        for h in range(HEADS):
            hs = slice(h * HD, (h + 1) * HD)
            c_old = c_ref[i, h]
            vcol = jnp.sum(vt_ref[hs, :] * onehot, axis=-1, keepdims=True)
            cq = jnp.sum(c_old * q_ref[i:i + 1, hs], axis=-1, keepdims=True)
            cqt_ref[hs, :] += cq * onehot
            co_ref[i, h] = dec_ref[b * HEADS + h] * c_old + vcol * gk_ref[i:i + 1, hs]


def _sc(dec_flat, state_c, layer, p, gk, vt, c_out_prev, *, bb):
    b = p.shape[0]
    in_specs = [
        pl.BlockSpec(memory_space=pltpu.SMEM),
        pl.BlockSpec((None, bb, HEADS, HD, HD), lambda j: (layer, j, 0, 0, 0)),
        pl.BlockSpec((bb, D_ML), lambda j: (j, 0)),
        pl.BlockSpec((bb, D_ML), lambda j: (j, 0)),
        pl.BlockSpec((D_ML, b), lambda j: (0, 0)),
    ]
    args = [dec_flat, state_c, p, gk, vt]
    aliases = {}
    if c_out_prev is not None:
        in_specs.append(pl.BlockSpec(memory_space=pl.ANY))
        args.append(c_out_prev)
        aliases = {5: 0}
    return pl.pallas_call(
        functools.partial(_sc_kernel, bb=bb),
        grid=(b // bb,),
        in_specs=in_specs,
        out_specs=[
            pl.BlockSpec((None, bb, HEADS, HD, HD), lambda j: (layer, j, 0, 0, 0)),
            pl.BlockSpec((D_ML, b), lambda j: (0, 0)),
        ],
        out_shape=[
            jax.ShapeDtypeStruct(state_c.shape, f32),
            jax.ShapeDtypeStruct((D_ML, b), f32),
        ],
        input_output_aliases=aliases,
        compiler_params=_cparams(("arbitrary",)),
        name="sample_state",
    )(*args)


def _spost_kernel(ws_ref, bs_ref, cqt_ref, p_ref, dec_ref, s_ref, stab_ref, x_ref, gate_ref, wout_ref,
                  gmh_ref, gfin_ref, xo_ref, merged_s, *, layer, final):
    for h in range(HEADS):
        hs = slice(h * HD, (h + 1) * HD)
        cq = jnp.concatenate([cqt_ref[h * HD + t * LANES:h * HD + (t + 1) * LANES, :].T
                              for t in range(HD // LANES)], axis=1)
        v = p_ref[:, OFF_V + h * HD:OFF_V + (h + 1) * HD]
        o_act = p_ref[:, OFF_O + h * HD:OFF_O + (h + 1) * HD]
        z_act = p_ref[:, OFF_ZML + h * HD:OFF_ZML + (h + 1) * HD]
        num = dec_ref[:, h:h + 1] * cq + s_ref[:, h:h + 1] * v
        hml = num / stab_ref[:, h:h + 1]
        y = _rms(o_act * hml, gmh_ref[layer:layer + 1, hs]) * z_act
        merged_s[:, hs] = y.astype(bf16)
    for g in range(GROUPS):
        u_act = p_ref[:, OFF_U + g * HD:OFF_U + (g + 1) * HD]
        vn = p_ref[:, OFF_VN + g * HD:OFF_VN + (g + 1) * HD]
        z_act = p_ref[:, OFF_ZCM + g * HD:OFF_ZCM + (g + 1) * HD]
        sp = ws_ref[layer, g] * vn + bs_ref[layer, g]
        merged_s[:, D_ML + g * HD:D_ML + (g + 1) * HD] = (u_act * sp * z_act).astype(bf16)
    out = jnp.dot(merged_s[...], wout_ref[...], preferred_element_type=f32)
    xn = x_ref[...] + gate_ref[...] * out
    if final:
        xn = _rms(xn, gfin_ref[...])
    xo_ref[...] = xn


def _spost(ws0, bs0, cqt, p, dec, s, stab, x2, mod, wout, g_mh, gfin, *, layer, final):
    b = p.shape[0]
    full = lambda i: (0, 0)
    smem = pl.BlockSpec(memory_space=pltpu.SMEM)
    small = pl.BlockSpec((b, HEADS), full)
    return pl.pallas_call(
        functools.partial(_spost_kernel, layer=layer, final=final),
        grid=(1,),
        in_specs=[
            smem, smem,
            pl.BlockSpec((D_ML, b), full),
            pl.BlockSpec((b, N_MAIN), full),
            small, small, small,
            pl.BlockSpec((b, D_MODEL), full),
            pl.BlockSpec((None, None, b, D_MODEL), lambda i: (layer, 2, MOD_SAMPLE_ROW, 0)),
            pl.BlockSpec((None, D_MODEL, D_MODEL), lambda i: (layer, 0, 0)),
            pl.BlockSpec(g_mh.shape, full),
            pl.BlockSpec((1, D_MODEL), full),
        ],
        out_specs=pl.BlockSpec((b, D_MODEL), full),
        out_shape=jax.ShapeDtypeStruct((b, D_MODEL), f32),
        scratch_shapes=[pltpu.VMEM((b, D_MODEL), bf16)],
        compiler_params=_cparams(("arbitrary",)),
        name="sample_post",
    )(ws0, bs0, cqt, p, dec, s, stab, x2, mod, wout, g_mh, gfin)


def kernel(x_prompt, x_sample, state_C, state_n, state_m, c_prompt, c_sample, g_norm, w_ada, b_ada, w_in,
           b_igate, b_fgate, g_mh, g_cmv, w_s, b_s, w_out, g_final):
    depth = w_in.shape[0]
    bp, seq, _ = x_prompt.shape
    bs = x_sample.shape[0]
    assert bs % 8 == 0 and bp <= 8

    c_all = jnp.concatenate([c_sample, c_prompt, jnp.zeros((8 - bp, D_MODEL), f32)], axis=0)
    mod = _ada(c_all, w_ada, b_ada)
    prompt_row_block = bs // 8
    wout = _cast_bf16(w_out)

    xp = x_prompt.reshape(bp * seq, D_MODEL)
    xs = x_sample.reshape(bs, D_MODEL)
    gfin = g_final.reshape(1, D_MODEL)
    state_n2 = state_n.reshape(depth, bs, D_ML)
    ws0 = w_s[:, :, 0, 0]
    bs0 = b_s[:, :, 0]

    hp, gp = _prenorm(xp, mod, g_norm, w_in, layer=0, tm=256, mod_row_block=prompt_row_block, mod_rows=8,
                      per_row_mod=False, rows_per_b=seq)
    prev = None
    cs_all = None
    ns, ms, vs = [], [], []
    for l in range(depth):
        final = l == depth - 1
        p = _in_proj(hp, w_in, g_cmv, layer=l, tm=1024, out_dtype=bf16)
        res = _mix(p, gp, xp, mod, wout, w_s, b_s, g_mh, gfin, b_igate, b_fgate, g_norm, w_in, prev,
                   layer=l, batch=bp, mod_row_block=prompt_row_block, cpb=2, final=final)
        if final:
            xp, *prev = res
        else:
            xp, hp, gp, *prev = res

        hs_, gs_ = _prenorm(xs, mod, g_norm, w_in, layer=l, tm=bs, mod_row_block=MOD_SAMPLE_ROW, mod_rows=bs,
                            per_row_mod=True, rows_per_b=None)
        p = _in_proj(hs_, w_in, g_cmv, layer=l, tm=bs, out_dtype=f32)
        gk, nn, mn, dec, s, stab, vt = _spre(p, gs_, b_igate, b_fgate, state_n2, state_m, layer=l)
        cs_all, cqt = _sc(dec.reshape(bs * HEADS), state_C, l, p, gk, vt, cs_all, bb=8)
        xs = _spost(ws0, bs0, cqt, p, dec, s, stab, xs, mod, wout, g_mh, gfin, layer=l, final=final)
        ns.append(nn.reshape(bs, HEADS, HD))
        ms.append(mn)
        vs.append(p[:, OFF_VN:OFF_VN + D_CM].reshape(bs, 1, D_CM))

    cp_all, np_all, mp_all = prev
    return (xp.reshape(bp, seq, D_MODEL), xs.reshape(bs, 1, D_MODEL),
            cp_all, np_all, mp_all[:, :, :HEADS, 0],
            cs_all, jnp.stack(ns), jnp.stack(ms), jnp.stack(vs))
```

```python
import functools

import jax
import jax.numpy as jnp
from jax import lax
from jax.experimental import pallas as pl
from jax.experimental.pallas import tpu as pltpu

f32 = jnp.float32
bf16 = jnp.bfloat16

D_MODEL = 2048
D_ML = 1024
D_CM = 1024
HEADS = 4
HD = 256
GROUPS = 4
CHUNK = 128
LANES = 128
KIND_W = 1024
N_KINDS = 8
N_MAIN = N_KINDS * KIND_W
N_GATES = 2 * HEADS
GATE_KIND = 5
EPS = 1e-6
OFF_Q, OFF_K, OFF_V, OFF_O, OFF_ZML, OFF_U, OFF_VN, OFF_ZCM = (i * KIND_W for i in range(N_KINDS))
OUT_CHUNKS = 2
MOD_SAMPLE_ROW = 0
VMEM_LIMIT = 56 * 1024 * 1024


NT_DIMS = (((1,), (1,)), ((), ()))


def _cparams(sem):
    return pltpu.CompilerParams(dimension_semantics=sem, vmem_limit_bytes=VMEM_LIMIT)


def _gate_rows_spec(layer):
    blk = GATE_KIND * KIND_W // LANES
    return pl.BlockSpec((None, LANES, D_MODEL), lambda *ids: (layer, blk, 0))


def _sigmoid(x):
    return 1.0 / (1.0 + jnp.exp(-x))


def _silu(x):
    return x * _sigmoid(x)


def _gelu(x):
    return 0.5 * x * (1.0 + jnp.tanh(0.7978845608028654 * (x + 0.044715 * (x * x * x))))


def _log_sigmoid(x):
    return jnp.minimum(x, 0.0) - jnp.log1p(jnp.exp(-jnp.abs(x)))


def _rms(x, g):
    return x * lax.rsqrt(jnp.mean(x * x, axis=-1, keepdims=True) + EPS) * g


def _layernorm(x, g):
    mu = jnp.mean(x, axis=-1, keepdims=True)
    xc = x - mu
    return xc * lax.rsqrt(jnp.mean(xc * xc, axis=-1, keepdims=True) + EPS) * g


def _gate_bias_row(bi_ref, bf_ref, layer):
    lane = lax.broadcasted_iota(jnp.int32, (1, LANES), 1)
    row = jnp.zeros((1, LANES), f32)
    for h in range(HEADS):
        row = jnp.where(lane == h, bi_ref[layer, h], row)
        row = jnp.where(lane == HEADS + h, bf_ref[layer, h], row)
    return row


def _modulated_norm(x, gn, sc, sh):
    return _rms(x, gn) * (1.0 + sc) + sh


def _ada_kernel(c_ref, w_ref, b_ref, o_ref):
    a = _silu(c_ref[...]).astype(bf16)
    o_ref[...] = jnp.dot(a, w_ref[...].astype(bf16), preferred_element_type=f32) + b_ref[...]


def _ada(c_all, w_ada, b_ada):
    depth, _, n3 = w_ada.shape
    rows = c_all.shape[0]
    tn = 512
    per = D_MODEL // tn
    return pl.pallas_call(
        _ada_kernel,
        grid=(depth, n3 // tn),
        in_specs=[
            pl.BlockSpec((rows, D_MODEL), lambda l, j: (0, 0)),
            pl.BlockSpec((None, D_MODEL, tn), lambda l, j: (l, 0, j)),
            pl.BlockSpec((None, 1, tn), lambda l, j: (l, 0, j)),
        ],
        out_specs=pl.BlockSpec((None, None, rows, tn), lambda l, j: (l, j // per, 0, j % per)),
        out_shape=jax.ShapeDtypeStruct((depth, 3, rows, D_MODEL), f32),
        compiler_params=_cparams(("arbitrary", "arbitrary")),
        name="ada",
    )(c_all, w_ada, b_ada.reshape(depth, 1, n3))


def _cast_kernel(w_ref, o_ref):
    o_ref[...] = w_ref[...].astype(bf16)


def _cast_bf16(w):
    depth, r, c = w.shape
    tr = 512
    return pl.pallas_call(
        _cast_kernel,
        grid=(depth, r // tr),
        in_specs=[pl.BlockSpec((None, tr, c), lambda l, i: (l, i, 0))],
        out_specs=pl.BlockSpec((None, tr, c), lambda l, i: (l, i, 0)),
        out_shape=jax.ShapeDtypeStruct(w.shape, bf16),
        compiler_params=_cparams(("arbitrary", "arbitrary")),
        name="cast_w_out",
    )(w)


def _prenorm_kernel(x_ref, sh_ref, sc_ref, gn_ref, wg_ref, h_ref, g_ref, *, layer, per_row_mod, rows_per_b):
    if per_row_mod:
        sh, sc = sh_ref[...], sc_ref[...]
    else:
        b = (pl.program_id(0) * x_ref.shape[0]) // rows_per_b
        sh, sc = sh_ref[pl.ds(b, 1), :], sc_ref[pl.ds(b, 1), :]
    h = _modulated_norm(x_ref[...], gn_ref[layer:layer + 1, :], sc, sh).astype(bf16)
    h_ref[...] = h
    g_ref[...] = lax.dot_general(h, wg_ref[...].astype(bf16), NT_DIMS, preferred_element_type=f32)


def _prenorm(x2, mod, g_norm, w_in_t, *, layer, tm, mod_row_block, mod_rows, per_row_mod, rows_per_b):
    m = x2.shape[0]
    mod_spec = lambda k: pl.BlockSpec((None, None, mod_rows, D_MODEL), lambda i: (layer, k, mod_row_block, 0))
    return pl.pallas_call(
        functools.partial(_prenorm_kernel, layer=layer, per_row_mod=per_row_mod, rows_per_b=rows_per_b),
        grid=(m // tm,),
        in_specs=[
            pl.BlockSpec((tm, D_MODEL), lambda i: (i, 0)),
            mod_spec(0),
            mod_spec(1),
            pl.BlockSpec(g_norm.shape, lambda i: (0, 0)),
            _gate_rows_spec(layer),
        ],
        out_specs=[pl.BlockSpec((tm, D_MODEL), lambda i: (i, 0)), pl.BlockSpec((tm, LANES), lambda i: (i, 0))],
        out_shape=[jax.ShapeDtypeStruct((m, D_MODEL), bf16), jax.ShapeDtypeStruct((m, LANES), f32)],
        compiler_params=_cparams(("arbitrary",)),
        name="prenorm",
    )(x2, mod, mod, g_norm, w_in_t)


def _in_kernel(h_ref, wm_ref, wx_ref, gcmv_ref, p_ref, wb_s, *, layer, tm, rc):
    j = pl.program_id(0)
    kc = 256
    nk = KIND_W // kc

    @pl.when(pl.program_id(1) == 0)
    def _():
        @pl.when(j < GATE_KIND)
        def _():
            for r in range(nk):
                wb_s[r * kc:(r + 1) * kc, :] = wm_ref[r * kc:(r + 1) * kc, :].astype(bf16)

        @pl.when(j >= GATE_KIND)
        def _():
            for r in range(nk - 1):
                wb_s[r * kc:(r + 1) * kc, :] = wm_ref[r * kc + N_GATES:(r + 1) * kc + N_GATES, :].astype(bf16)
            last = jnp.concatenate([wm_ref[(nk - 1) * kc + N_GATES:, :], wx_ref[...]], axis=0)
            wb_s[(nk - 1) * kc:, :] = last.astype(bf16)

    def vn_act(acc):
        g = gcmv_ref[layer:layer + 1, :]
        return jnp.concatenate([_layernorm(_gelu(acc[:, t * HD:(t + 1) * HD]), g[:, t * HD:(t + 1) * HD])
                                for t in range(GROUPS)], axis=1)

    def run(act):
        for r in range(tm // rc):
            acc = lax.dot_general(h_ref[r * rc:(r + 1) * rc, :], wb_s[...], NT_DIMS, preferred_element_type=f32)
            p_ref[r * rc:(r + 1) * rc, :] = act(acc).astype(p_ref.dtype)

    pl.when((j == 0) | (j == 2))(lambda: run(lambda a: a))
    pl.when(j == 1)(lambda: run(lambda a: a * (HD ** -0.5)))
    pl.when(j == 3)(lambda: run(_sigmoid))
    pl.when((j == 4) | (j == 7))(lambda: run(_silu))
    pl.when(j == 5)(lambda: run(_gelu))
    pl.when(j == 6)(lambda: run(vn_act))


def _in_proj(h, w_in_t, g_cmv, *, layer, tm, out_dtype):
    m = h.shape[0]
    rc = min(tm, 256)
    return pl.pallas_call(
        functools.partial(_in_kernel, layer=layer, tm=tm, rc=rc),
        grid=(N_KINDS, m // tm),
        in_specs=[
            pl.BlockSpec((tm, D_MODEL), lambda j, i: (i, 0)),
            pl.BlockSpec((None, KIND_W, D_MODEL), lambda j, i: (layer, j, 0)),
            pl.BlockSpec((None, N_GATES, D_MODEL), lambda j, i: (layer, (j + 1) * (KIND_W // N_GATES), 0)),
            pl.BlockSpec(g_cmv.shape, lambda j, i: (0, 0)),
        ],
        out_specs=pl.BlockSpec((tm, KIND_W), lambda j, i: (i, j)),
        out_shape=jax.ShapeDtypeStruct((m, N_MAIN), out_dtype),
        scratch_shapes=[pltpu.VMEM((KIND_W, D_MODEL), bf16)],
        compiler_params=_cparams(("arbitrary", "arbitrary")),
        name="in_proj",
    )(h, w_in_t, w_in_t, g_cmv)


def _mix_kernel(bi_ref, bf_ref, p_ref, g_ref, x_ref, gate_ref, wout_ref, ws_ref, bs_ref, gmh_ref, gfin_ref,
                *rest, layer, cpb, final):
    if final:
        xo_ref, c_ref, n_ref, m_ref, merged_s = rest[-5:]
    else:
        sh_ref, sc_ref, gn_ref, wg_ref = rest[:4]
        xo_ref, hn_ref, gn_out_ref, c_ref, n_ref, m_ref, merged_s = rest[-7:]
    b = pl.program_id(0)

    @pl.when(pl.program_id(1) == 0)
    def _():
        c_ref[...] = jnp.zeros_like(c_ref)
        n_ref[...] = jnp.zeros_like(n_ref)
        m_ref[...] = jnp.zeros_like(m_ref)

    row = lax.broadcasted_iota(jnp.int32, (CHUNK, CHUNK), 0)
    col = lax.broadcasted_iota(jnp.int32, (CHUNK, CHUNK), 1)
    causal = col <= row
    tril1 = causal.astype(f32)
    bias_row = _gate_bias_row(bi_ref, bf_ref, layer)
    nt = NT_DIMS
    ws_tril = [jnp.where(causal, ws_ref[g], 0.0).astype(bf16) for g in range(GROUPS)]
    bs_col = [jnp.sum(jnp.where(row == col, bs_ref[g:g + 1, :], 0.0), axis=-1, keepdims=True)
              for g in range(GROUPS)]

    def project_out(rows):
        out = jnp.dot(merged_s[rows, :], wout_ref[...], preferred_element_type=f32)
        xn = x_ref[rows, :] + gate_ref[pl.ds(b, 1), :] * out
        if final:
            xo_ref[rows, :] = _rms(xn, gfin_ref[...])
        else:
            xo_ref[rows, :] = xn
            hn = _modulated_norm(xn, gn_ref[layer + 1:layer + 2, :], sc_ref[pl.ds(b, 1), :],
                                 sh_ref[pl.ds(b, 1), :]).astype(bf16)
            hn_ref[rows, :] = hn
            gn_out_ref[rows, :] = lax.dot_general(hn, wg_ref[...].astype(bf16), NT_DIMS,
                                                  preferred_element_type=f32)

    for ci in range(cpb):
        rows = slice(ci * CHUNK, (ci + 1) * CHUNK)
        gt = g_ref[rows, :] + bias_row
        lf = _log_sigmoid(gt)
        cum = jnp.dot(tril1, lf, precision=lax.Precision.HIGHEST, preferred_element_type=f32)
        gt_t = gt.T
        cum_t = cum.T
        for h in range(HEADS):
            hs = slice(h * HD, (h + 1) * HD)
            qb = p_ref[rows, OFF_Q + h * HD:OFF_Q + (h + 1) * HD]
            kb = p_ref[rows, OFF_K + h * HD:OFF_K + (h + 1) * HD]
            vb = p_ref[rows, OFF_V + h * HD:OFF_V + (h + 1) * HD]
            ig_col = gt[:, h:h + 1]
            ig_row = gt_t[h:h + 1, :]
            b_col = cum[:, HEADS + h:HEADS + h + 1]
            b_row = cum_t[HEADS + h:HEADS + h + 1, :]
            m_old = m_ref[h:h + 1, 0:1]
            c_old = c_ref[h]
            n_old = n_ref[h:h + 1, :]

            dmat = jnp.where(causal, b_col - b_row + ig_row, -jnp.inf)
            a_col = b_col + m_old
            mt = jnp.maximum(a_col, jnp.max(dmat, axis=-1, keepdims=True))
            w_intra = jnp.exp(dmat - mt)
            w_inter = jnp.exp(a_col - mt)
            s = lax.dot_general(qb, kb, nt, preferred_element_type=f32) * w_intra
            cq = lax.dot_general(qb, c_old.astype(bf16), nt, preferred_element_type=f32)
            num = w_inter * cq + jnp.dot(s.astype(bf16), vb, preferred_element_type=f32)
            den = (w_inter * jnp.sum(qb.astype(f32) * n_old, axis=-1, keepdims=True)
                   + jnp.sum(s, axis=-1, keepdims=True))
            hml = num / jnp.maximum(jnp.abs(den), jnp.exp(-mt))

            m_new = mt[CHUNK - 1:CHUNK, :]
            b_last = b_col[CHUNK - 1:CHUNK, :]
            g_end = jnp.exp(b_last - b_col + ig_col - m_new)
            decay = jnp.exp(b_last + m_old - m_new)
            gv_t = (g_end * vb.astype(f32)).T.astype(bf16)
            c_ref[h] = decay * c_old + jnp.dot(gv_t, kb, preferred_element_type=f32)
            n_ref[h:h + 1, :] = decay * n_old + jnp.sum(g_end * kb.astype(f32), axis=0, keepdims=True)
            m_ref[h:h + 1, :] = jnp.broadcast_to(m_new, (1, m_ref.shape[1]))

            o_act = p_ref[rows, OFF_O + h * HD:OFF_O + (h + 1) * HD].astype(f32)
            z_act = p_ref[rows, OFF_ZML + h * HD:OFF_ZML + (h + 1) * HD].astype(f32)
            y = _rms(o_act * hml, gmh_ref[layer:layer + 1, hs]) * z_act
            merged_s[rows, hs] = y.astype(bf16)

        for g in range(GROUPS):
            u_act = p_ref[rows, OFF_U + g * HD:OFF_U + (g + 1) * HD].astype(f32)
            vn = p_ref[rows, OFF_VN + g * HD:OFF_VN + (g + 1) * HD]
            z_act = p_ref[rows, OFF_ZCM + g * HD:OFF_ZCM + (g + 1) * HD].astype(f32)
            sp = jnp.dot(ws_tril[g], vn, preferred_element_type=f32) + bs_col[g]
            merged_s[rows, D_ML + g * HD:D_ML + (g + 1) * HD] = (u_act * sp * z_act).astype(bf16)

        if ci % OUT_CHUNKS == OUT_CHUNKS - 1:
            project_out(slice((ci + 1 - OUT_CHUNKS) * CHUNK, (ci + 1) * CHUNK))


def _mix(p, g, x2, mod, wout, w_s, b_s, g_mh, gfin, b_ig, b_fg, g_norm, w_in_t, prev, *,
         layer, batch, mod_row_block, cpb, final):
    m = x2.shape[0]
    depth = w_s.shape[0]
    ts = cpb * CHUNK
    steps = m // batch // ts
    tok = lambda b, c: (b * steps + c, 0)
    full2 = lambda b, c: (0, 0)
    smem = pl.BlockSpec(memory_space=pltpu.SMEM)
    mod_spec = lambda lyr, k: pl.BlockSpec((None, None, 8, D_MODEL), lambda b, c: (lyr, k, mod_row_block, 0))
    in_specs = [
        smem, smem,
        pl.BlockSpec((ts, N_MAIN), tok),
        pl.BlockSpec((ts, LANES), tok),
        pl.BlockSpec((ts, D_MODEL), tok),
        mod_spec(layer, 2),
        pl.BlockSpec((None, D_MODEL, D_MODEL), lambda b, c: (layer, 0, 0), pipeline_mode=pl.Buffered(1)),
        pl.BlockSpec((None, GROUPS, CHUNK, CHUNK), lambda b, c: (layer, 0, 0, 0)),
        pl.BlockSpec((None, GROUPS, CHUNK), lambda b, c: (layer, 0, 0)),
        pl.BlockSpec(g_mh.shape, full2),
        pl.BlockSpec((1, D_MODEL), full2),
    ]
    args = [b_ig, b_fg, p, g, x2, mod, wout, w_s, b_s, g_mh, gfin]
    out_specs = [pl.BlockSpec((ts, D_MODEL), tok)]
    out_shape = [jax.ShapeDtypeStruct((m, D_MODEL), f32)]
    if not final:
        in_specs += [
            mod_spec(layer + 1, 0), mod_spec(layer + 1, 1),
            pl.BlockSpec(g_norm.shape, full2),
            _gate_rows_spec(layer + 1),
        ]
        args += [mod, mod, g_norm, w_in_t]
        out_specs += [pl.BlockSpec((ts, D_MODEL), tok), pl.BlockSpec((ts, LANES), tok)]
        out_shape += [jax.ShapeDtypeStruct((m, D_MODEL), bf16), jax.ShapeDtypeStruct((m, LANES), f32)]
    n_state_out = len(out_specs)
    out_specs += [
        pl.BlockSpec((None, None, HEADS, HD, HD), lambda b, c: (layer, b, 0, 0, 0)),
        pl.BlockSpec((None, None, HEADS, HD), lambda b, c: (layer, b, 0, 0)),
        pl.BlockSpec((None, None, 8, LANES), lambda b, c: (layer, b, 0, 0)),
    ]
    out_shape += [
        jax.ShapeDtypeStruct((depth, batch, HEADS, HD, HD), f32),
        jax.ShapeDtypeStruct((depth, batch, HEADS, HD), f32),
        jax.ShapeDtypeStruct((depth, batch, 8, LANES), f32),
    ]
    aliases = {}
    if prev is not None:
        for t, arr in enumerate(prev):
            aliases[len(args)] = n_state_out + t
            in_specs.append(pl.BlockSpec(memory_space=pl.ANY))
            args.append(arr)
    kern = functools.partial(_mix_kernel, layer=layer, cpb=cpb, final=final)
    return pl.pallas_call(
        kern,
        grid=(batch, steps),
        in_specs=in_specs,
        out_specs=out_specs,
        out_shape=out_shape,
        scratch_shapes=[pltpu.VMEM((ts, D_MODEL), bf16)],
        input_output_aliases=aliases,
        compiler_params=_cparams(("arbitrary", "arbitrary")),
        name="mix",
    )(*args)


def _spre_kernel(bi_ref, bf_ref, p_ref, g_ref, n_ref, m_ref,
                 gk_ref, nn_ref, mn_ref, dec_ref, s_ref, stab_ref, vt_ref, *, layer):
    gt = g_ref[...] + _gate_bias_row(bi_ref, bf_ref, layer)
    lf = _log_sigmoid(gt)
    for h in range(HEADS):
        hs = slice(h * HD, (h + 1) * HD)
        q = p_ref[:, OFF_Q + h * HD:OFF_Q + (h + 1) * HD]
        k = p_ref[:, OFF_K + h * HD:OFF_K + (h + 1) * HD]
        ig = gt[:, h:h + 1]
        a = lf[:, HEADS + h:HEADS + h + 1] + m_ref[:, h:h + 1]
        mt = jnp.maximum(a, ig)
        w_intra = jnp.exp(ig - mt)
        w_inter = jnp.exp(a - mt)
        n_old = n_ref[:, hs]
        s = jnp.sum(q * k, axis=-1, keepdims=True) * w_intra
        den = w_inter * jnp.sum(n_old * q, axis=-1, keepdims=True) + s
        gk = w_intra * k
        gk_ref[:, hs] = gk
        nn_ref[:, hs] = w_inter * n_old + gk
        mn_ref[:, h:h + 1] = mt
        dec_ref[:, h:h + 1] = w_inter
        s_ref[:, h:h + 1] = s
        stab_ref[:, h:h + 1] = jnp.maximum(jnp.abs(den), jnp.exp(-mt))
    for t in range(D_ML // LANES):
        vt_ref[t * LANES:(t + 1) * LANES, :] = p_ref[:, OFF_V + t * LANES:OFF_V + (t + 1) * LANES].T


def _spre(p, g, b_ig, b_fg, state_n, state_m, *, layer):
    b = p.shape[0]
    full = lambda i: (0, 0)
    smem = pl.BlockSpec(memory_space=pltpu.SMEM)
    small = jax.ShapeDtypeStruct((b, HEADS), f32)
    wide = jax.ShapeDtypeStruct((b, D_ML), f32)
    small_spec = pl.BlockSpec((b, HEADS), full)
    wide_spec = pl.BlockSpec((b, D_ML), full)
    return pl.pallas_call(
        functools.partial(_spre_kernel, layer=layer),
        grid=(1,),
        in_specs=[
            smem, smem,
            pl.BlockSpec((b, 3 * D_ML), full),
            pl.BlockSpec((b, LANES), full),
            pl.BlockSpec((None, b, D_ML), lambda i: (layer, 0, 0)),
            pl.BlockSpec((None, b, HEADS), lambda i: (layer, 0, 0)),
        ],
        out_specs=[wide_spec, wide_spec, small_spec, small_spec, small_spec, small_spec,
                   pl.BlockSpec((D_ML, b), full)],
        out_shape=[wide, wide, small, small, small, small, jax.ShapeDtypeStruct((D_ML, b), f32)],
        compiler_params=_cparams(("arbitrary",)),
        name="sample_pre",
    )(b_ig, b_fg, p, g, state_n, state_m)


def _sc_kernel(dec_ref, c_ref, q_ref, gk_ref, vt_ref, *rest, bb):
    co_ref, cqt_ref = rest[-2:]
    j = pl.program_id(0)

    @pl.when(j == 0)
    def _():
        cqt_ref[...] = jnp.zeros_like(cqt_ref)

    lane = lax.broadcasted_iota(jnp.int32, (HD, cqt_ref.shape[1]), 1)
    for i in range(bb):
        b = j * bb + i
        onehot = (lane == b).astype(f32)
        for h in range(HEADS):
            hs = slice(h * HD, (h + 1) * HD)
            c_old = c_ref[i, h]
            vcol = jnp.sum(vt_ref[hs, :] * onehot, axis=-1, keepdims=True)
            cq = jnp.sum(c_old * q_ref[i:i + 1, hs], axis=-1, keepdims=True)
            cqt_ref[hs, :] += cq * onehot
            co_ref[i, h] = dec_ref[b * HEADS + h] * c_old + vcol * gk_ref[i:i + 1, hs]


def _sc(dec_flat, state_c, layer, p, gk, vt, c_out_prev, *, bb):
    b = p.shape[0]
    in_specs = [
        pl.BlockSpec(memory_space=pltpu.SMEM),
        pl.BlockSpec((None, bb, HEADS, HD, HD), lambda j: (layer, j, 0, 0, 0)),
        pl.BlockSpec((bb, D_ML), lambda j: (j, 0)),
        pl.BlockSpec((bb, D_ML), lambda j: (j, 0)),
        pl.BlockSpec((D_ML, b), lambda j: (0, 0)),
    ]
    args = [dec_flat, state_c, p, gk, vt]
    aliases = {}
    if c_out_prev is not None:
        in_specs.append(pl.BlockSpec(memory_space=pl.ANY))
        args.append(c_out_prev)
        aliases = {5: 0}
    return pl.pallas_call(
        functools.partial(_sc_kernel, bb=bb),
        grid=(b // bb,),
        in_specs=in_specs,
        out_specs=[
            pl.BlockSpec((None, bb, HEADS, HD, HD), lambda j: (layer, j, 0, 0, 0)),
            pl.BlockSpec((D_ML, b), lambda j: (0, 0)),
        ],
        out_shape=[
            jax.ShapeDtypeStruct(state_c.shape, f32),
            jax.ShapeDtypeStruct((D_ML, b), f32),
        ],
        input_output_aliases=aliases,
        compiler_params=_cparams(("arbitrary",)),
        name="sample_state",
    )(*args)


def _spost_kernel(ws_ref, bs_ref, cqt_ref, p_ref, dec_ref, s_ref, stab_ref, x_ref, gate_ref, wout_ref,
                  gmh_ref, gfin_ref, xo_ref, merged_s, *, layer, final):
    for h in range(HEADS):
        hs = slice(h * HD, (h + 1) * HD)
        cq = jnp.concatenate([cqt_ref[h * HD + t * LANES:h * HD + (t + 1) * LANES, :].T
                              for t in range(HD // LANES)], axis=1)
        v = p_ref[:, OFF_V + h * HD:OFF_V + (h + 1) * HD]
        o_act = p_ref[:, OFF_O + h * HD:OFF_O + (h + 1) * HD]
        z_act = p_ref[:, OFF_ZML + h * HD:OFF_ZML + (h + 1) * HD]
        num = dec_ref[:, h:h + 1] * cq + s_ref[:, h:h + 1] * v
        hml = num / stab_ref[:, h:h + 1]
        y = _rms(o_act * hml, gmh_ref[layer:layer + 1, hs]) * z_act
        merged_s[:, hs] = y.astype(bf16)
    for g in range(GROUPS):
        u_act = p_ref[:, OFF_U + g * HD:OFF_U + (g + 1) * HD]
        vn = p_ref[:, OFF_VN + g * HD:OFF_VN + (g + 1) * HD]
        z_act = p_ref[:, OFF_ZCM + g * HD:OFF_ZCM + (g + 1) * HD]
        sp = ws_ref[layer, g] * vn + bs_ref[layer, g]
        merged_s[:, D_ML + g * HD:D_ML + (g + 1) * HD] = (u_act * sp * z_act).astype(bf16)
    out = jnp.dot(merged_s[...], wout_ref[...], preferred_element_type=f32)
    xn = x_ref[...] + gate_ref[...] * out
    if final:
        xn = _rms(xn, gfin_ref[...])
    xo_ref[...] = xn


def _spost(ws0, bs0, cqt, p, dec, s, stab, x2, mod, wout, g_mh, gfin, *, layer, final):
    b = p.shape[0]
    full = lambda i: (0, 0)
    smem = pl.BlockSpec(memory_space=pltpu.SMEM)
    small = pl.BlockSpec((b, HEADS), full)
    return pl.pallas_call(
        functools.partial(_spost_kernel, layer=layer, final=final),
        grid=(1,),
        in_specs=[
            smem, smem,
            pl.BlockSpec((D_ML, b), full),
            pl.BlockSpec((b, N_MAIN), full),
            small, small, small,
            pl.BlockSpec((b, D_MODEL), full),
            pl.BlockSpec((None, None, b, D_MODEL), lambda i: (layer, 2, MOD_SAMPLE_ROW, 0)),
            pl.BlockSpec((None, D_MODEL, D_MODEL), lambda i: (layer, 0, 0)),
            pl.BlockSpec(g_mh.shape, full),
            pl.BlockSpec((1, D_MODEL), full),
        ],
        out_specs=pl.BlockSpec((b, D_MODEL), full),
        out_shape=jax.ShapeDtypeStruct((b, D_MODEL), f32),
        scratch_shapes=[pltpu.VMEM((b, D_MODEL), bf16)],
        compiler_params=_cparams(("arbitrary",)),
        name="sample_post",
    )(ws0, bs0, cqt, p, dec, s, stab, x2, mod, wout, g_mh, gfin)


def kernel(x_prompt, x_sample, state_C, state_n, state_m, c_prompt, c_sample, g_norm, w_ada, b_ada, w_in,
           b_igate, b_fgate, g_mh, g_cmv, w_s, b_s, w_out, g_final):
    depth = w_in.shape[0]
    bp, seq, _ = x_prompt.shape
    bs = x_sample.shape[0]
    assert bs % 8 == 0 and bp <= 8

    c_all = jnp.concatenate([c_sample, c_prompt, jnp.zeros((8 - bp, D_MODEL), f32)], axis=0)
    mod = _ada(c_all, w_ada, b_ada)
    prompt_row_block = bs // 8
    wout = _cast_bf16(w_out)

    xp = x_prompt.reshape(bp * seq, D_MODEL)
    xs = x_sample.reshape(bs, D_MODEL)
    gfin = g_final.reshape(1, D_MODEL)
    state_n2 = state_n.reshape(depth, bs, D_ML)
    ws0 = w_s[:, :, 0, 0]
    bs0 = b_s[:, :, 0]
    w_in = jnp.swapaxes(w_in, 1, 2)

    hp, gp = _prenorm(xp, mod, g_norm, w_in, layer=0, tm=512, mod_row_block=prompt_row_block, mod_rows=8,
                      per_row_mod=False, rows_per_b=seq)
    prev = None
    cs_all = None
    ns, ms, vs = [], [], []
    for l in range(depth):
        final = l == depth - 1
        p = _in_proj(hp, w_in, g_cmv, layer=l, tm=1024, out_dtype=bf16)
        res = _mix(p, gp, xp, mod, wout, w_s, b_s, g_mh, gfin, b_igate, b_fgate, g_norm, w_in, prev,
                   layer=l, batch=bp, mod_row_block=prompt_row_block, cpb=4, final=final)
        if final:
            xp, *prev = res
        else:
            xp, hp, gp, *prev = res

        hs_, gs_ = _prenorm(xs, mod, g_norm, w_in, layer=l, tm=bs, mod_row_block=MOD_SAMPLE_ROW, mod_rows=bs,
                            per_row_mod=True, rows_per_b=None)
        p = _in_proj(hs_, w_in, g_cmv, layer=l, tm=bs, out_dtype=f32)
        gk, nn, mn, dec, s, stab, vt = _spre(p, gs_, b_igate, b_fgate, state_n2, state_m, layer=l)
        cs_all, cqt = _sc(dec.reshape(bs * HEADS), state_C, l, p, gk, vt, cs_all, bb=8)
        xs = _spost(ws0, bs0, cqt, p, dec, s, stab, xs, mod, wout, g_mh, gfin, layer=l, final=final)
        ns.append(nn.reshape(bs, HEADS, HD))
        ms.append(mn)
        vs.append(p[:, OFF_VN:OFF_VN + D_CM].reshape(bs, 1, D_CM))

    cp_all, np_all, mp_all = prev
    return (xp.reshape(bp, seq, D_MODEL), xs.reshape(bs, 1, D_MODEL),
            cp_all, np_all, mp_all[:, :, :HEADS, 0],
            cs_all, jnp.stack(ns), jnp.stack(ms), jnp.stack(vs))
```

```python
import functools

import jax
import jax.numpy as jnp
from jax import lax
from jax.experimental import pallas as pl
from jax.experimental.pallas import tpu as pltpu

f32 = jnp.float32
bf16 = jnp.bfloat16

D_MODEL = 2048
D_ML = 1024
D_CM = 1024
HEADS = 4
HD = 256
GROUPS = 4
CHUNK = 128
LANES = 128
KIND_W = 1024
N_KINDS = 8
N_MAIN = N_KINDS * KIND_W
N_GATES = 2 * HEADS
GATE_KIND = 5
EPS = 1e-6
OFF_Q, OFF_K, OFF_V, OFF_O, OFF_ZML, OFF_U, OFF_VN, OFF_ZCM = (i * KIND_W for i in range(N_KINDS))
OUT_CHUNKS = 2
MOD_SAMPLE_ROW = 0
VMEM_LIMIT = 56 * 1024 * 1024


NT_DIMS = (((1,), (1,)), ((), ()))


def _cparams(sem):
    return pltpu.CompilerParams(dimension_semantics=sem, vmem_limit_bytes=VMEM_LIMIT)


def _gate_rows_spec(layer):
    blk = GATE_KIND * KIND_W // LANES
    return pl.BlockSpec((None, LANES, D_MODEL), lambda *ids: (layer, blk, 0))


def _sigmoid(x):
    return 1.0 / (1.0 + jnp.exp(-x))


def _silu(x):
    return x * _sigmoid(x)


def _gelu(x):
    return 0.5 * x * (1.0 + jnp.tanh(0.7978845608028654 * (x + 0.044715 * (x * x * x))))


def _log_sigmoid(x):
    return jnp.minimum(x, 0.0) - jnp.log1p(jnp.exp(-jnp.abs(x)))


def _rms(x, g):
    return x * lax.rsqrt(jnp.mean(x * x, axis=-1, keepdims=True) + EPS) * g


def _layernorm(x, g):
    mu = jnp.mean(x, axis=-1, keepdims=True)
    xc = x - mu
    return xc * lax.rsqrt(jnp.mean(xc * xc, axis=-1, keepdims=True) + EPS) * g


def _gate_bias_row(bi_ref, bf_ref, layer):
    lane = lax.broadcasted_iota(jnp.int32, (1, LANES), 1)
    row = jnp.zeros((1, LANES), f32)
    for h in range(HEADS):
        row = jnp.where(lane == h, bi_ref[layer, h], row)
        row = jnp.where(lane == HEADS + h, bf_ref[layer, h], row)
    return row


def _modulated_norm(x, gn, sc, sh):
    return _rms(x, gn) * (1.0 + sc) + sh


def _ada_kernel(c_ref, w_ref, b_ref, o_ref):
    a = _silu(c_ref[...]).astype(bf16)
    o_ref[...] = jnp.dot(a, w_ref[...].astype(bf16), preferred_element_type=f32) + b_ref[...]


def _ada(c_all, w_ada, b_ada):
    depth, _, n3 = w_ada.shape
    rows = c_all.shape[0]
    tn = 512
    per = D_MODEL // tn
    return pl.pallas_call(
        _ada_kernel,
        grid=(depth, n3 // tn),
        in_specs=[
            pl.BlockSpec((rows, D_MODEL), lambda l, j: (0, 0)),
            pl.BlockSpec((None, D_MODEL, tn), lambda l, j: (l, 0, j)),
            pl.BlockSpec((None, 1, tn), lambda l, j: (l, 0, j)),
        ],
        out_specs=pl.BlockSpec((None, None, rows, tn), lambda l, j: (l, j // per, 0, j % per)),
        out_shape=jax.ShapeDtypeStruct((depth, 3, rows, D_MODEL), f32),
        compiler_params=_cparams(("arbitrary", "arbitrary")),
        name="ada",
    )(c_all, w_ada, b_ada.reshape(depth, 1, n3))


def _cast_kernel(w_ref, o_ref):
    o_ref[...] = w_ref[...].astype(bf16)


def _cast_bf16(w):
    depth, r, c = w.shape
    tr = 512
    return pl.pallas_call(
        _cast_kernel,
        grid=(depth, r // tr),
        in_specs=[pl.BlockSpec((None, tr, c), lambda l, i: (l, i, 0))],
        out_specs=pl.BlockSpec((None, tr, c), lambda l, i: (l, i, 0)),
        out_shape=jax.ShapeDtypeStruct(w.shape, bf16),
        compiler_params=_cparams(("arbitrary", "arbitrary")),
        name="cast_w_out",
    )(w)


def _prenorm_kernel(x_ref, sh_ref, sc_ref, gn_ref, wg_ref, h_ref, g_ref, *, layer, per_row_mod, rows_per_b):
    if per_row_mod:
        sh, sc = sh_ref[...], sc_ref[...]
    else:
        b = (pl.program_id(0) * x_ref.shape[0]) // rows_per_b
        sh, sc = sh_ref[pl.ds(b, 1), :], sc_ref[pl.ds(b, 1), :]
    h = _modulated_norm(x_ref[...], gn_ref[layer:layer + 1, :], sc, sh).astype(bf16)
    h_ref[...] = h
    g_ref[...] = lax.dot_general(h, wg_ref[...].astype(bf16), NT_DIMS, preferred_element_type=f32)


def _prenorm(x2, mod, g_norm, w_in_t, *, layer, tm, mod_row_block, mod_rows, per_row_mod, rows_per_b):
    m = x2.shape[0]
    mod_spec = lambda k: pl.BlockSpec((None, None, mod_rows, D_MODEL), lambda i: (layer, k, mod_row_block, 0))
    return pl.pallas_call(
        functools.partial(_prenorm_kernel, layer=layer, per_row_mod=per_row_mod, rows_per_b=rows_per_b),
        grid=(m // tm,),
        in_specs=[
            pl.BlockSpec((tm, D_MODEL), lambda i: (i, 0)),
            mod_spec(0),
            mod_spec(1),
            pl.BlockSpec(g_norm.shape, lambda i: (0, 0)),
            _gate_rows_spec(layer),
        ],
        out_specs=[pl.BlockSpec((tm, D_MODEL), lambda i: (i, 0)), pl.BlockSpec((tm, LANES), lambda i: (i, 0))],
        out_shape=[jax.ShapeDtypeStruct((m, D_MODEL), bf16), jax.ShapeDtypeStruct((m, LANES), f32)],
        compiler_params=_cparams(("arbitrary",)),
        name="prenorm",
    )(x2, mod, mod, g_norm, w_in_t)


def _in_kernel(h_ref, wm_ref, wx_ref, gcmv_ref, p_ref, wb_s, *, layer, tm, rc):
    j = pl.program_id(0)
    kc = 256
    nk = KIND_W // kc

    @pl.when(pl.program_id(1) == 0)
    def _():
        @pl.when(j < GATE_KIND)
        def _():
            for r in range(nk):
                wb_s[r * kc:(r + 1) * kc, :] = wm_ref[r * kc:(r + 1) * kc, :].astype(bf16)

        @pl.when(j >= GATE_KIND)
        def _():
            for r in range(nk - 1):
                wb_s[r * kc:(r + 1) * kc, :] = wm_ref[r * kc + N_GATES:(r + 1) * kc + N_GATES, :].astype(bf16)
            last = jnp.concatenate([wm_ref[(nk - 1) * kc + N_GATES:, :], wx_ref[...]], axis=0)
            wb_s[(nk - 1) * kc:, :] = last.astype(bf16)

    def vn_act(acc):
        g = gcmv_ref[layer:layer + 1, :]
        return jnp.concatenate([_layernorm(_gelu(acc[:, t * HD:(t + 1) * HD]), g[:, t * HD:(t + 1) * HD])
                                for t in range(GROUPS)], axis=1)

    def run(act):
        for r in range(tm // rc):
            acc = lax.dot_general(h_ref[r * rc:(r + 1) * rc, :], wb_s[...], NT_DIMS, preferred_element_type=f32)
            p_ref[r * rc:(r + 1) * rc, :] = act(acc).astype(p_ref.dtype)

    pl.when((j == 0) | (j == 2))(lambda: run(lambda a: a))
    pl.when(j == 1)(lambda: run(lambda a: a * (HD ** -0.5)))
    pl.when(j == 3)(lambda: run(_sigmoid))
    pl.when((j == 4) | (j == 7))(lambda: run(_silu))
    pl.when(j == 5)(lambda: run(_gelu))
    pl.when(j == 6)(lambda: run(vn_act))


def _in_proj(h, w_in_t, g_cmv, *, layer, tm, out_dtype):
    m = h.shape[0]
    rc = min(tm, 256)
    return pl.pallas_call(
        functools.partial(_in_kernel, layer=layer, tm=tm, rc=rc),
        grid=(N_KINDS, m // tm),
        in_specs=[
            pl.BlockSpec((tm, D_MODEL), lambda j, i: (i, 0)),
            pl.BlockSpec((None, KIND_W, D_MODEL), lambda j, i: (layer, j, 0)),
            pl.BlockSpec((None, N_GATES, D_MODEL), lambda j, i: (layer, (j + 1) * (KIND_W // N_GATES), 0)),
            pl.BlockSpec(g_cmv.shape, lambda j, i: (0, 0)),
        ],
        out_specs=pl.BlockSpec((tm, KIND_W), lambda j, i: (i, j)),
        out_shape=jax.ShapeDtypeStruct((m, N_MAIN), out_dtype),
        scratch_shapes=[pltpu.VMEM((KIND_W, D_MODEL), bf16)],
        compiler_params=_cparams(("arbitrary", "arbitrary")),
        name="in_proj",
    )(h, w_in_t, w_in_t, g_cmv)


def _mix_kernel(bi_ref, bf_ref, p_ref, g_ref, x_ref, gate_ref, wout_ref, ws_ref, bs_ref, gmh_ref, gfin_ref,
                *rest, layer, cpb, final):
    if final:
        xo_ref, c_ref, n_ref, m_ref, merged_s = rest[-5:]
    else:
        sh_ref, sc_ref, gn_ref, wg_ref = rest[:4]
        xo_ref, hn_ref, gn_out_ref, c_ref, n_ref, m_ref, merged_s = rest[-7:]
    b = pl.program_id(0)

    @pl.when(pl.program_id(1) == 0)
    def _():
        c_ref[...] = jnp.zeros_like(c_ref)
        n_ref[...] = jnp.zeros_like(n_ref)
        m_ref[...] = jnp.zeros_like(m_ref)

    row = lax.broadcasted_iota(jnp.int32, (CHUNK, CHUNK), 0)
    col = lax.broadcasted_iota(jnp.int32, (CHUNK, CHUNK), 1)
    causal = col <= row
    tril1 = causal.astype(f32)
    bias_row = _gate_bias_row(bi_ref, bf_ref, layer)
    nt = NT_DIMS
    ws_tril = [jnp.where(causal, ws_ref[g], 0.0).astype(bf16) for g in range(GROUPS)]
    bs_col = [jnp.sum(jnp.where(row == col, bs_ref[g:g + 1, :], 0.0), axis=-1, keepdims=True)
              for g in range(GROUPS)]

    def project_out(rows):
        out = jnp.dot(merged_s[rows, :], wout_ref[...], preferred_element_type=f32)
        xn = x_ref[rows, :] + gate_ref[pl.ds(b, 1), :] * out
        if final:
            xo_ref[rows, :] = _rms(xn, gfin_ref[...])
        else:
            xo_ref[rows, :] = xn
            hn = _modulated_norm(xn, gn_ref[layer + 1:layer + 2, :], sc_ref[pl.ds(b, 1), :],
                                 sh_ref[pl.ds(b, 1), :]).astype(bf16)
            hn_ref[rows, :] = hn
            gn_out_ref[rows, :] = lax.dot_general(hn, wg_ref[...].astype(bf16), NT_DIMS,
                                                  preferred_element_type=f32)

    gts = [g_ref[ci * CHUNK:(ci + 1) * CHUNK, :] + bias_row for ci in range(cpb)]
    lfs = [_log_sigmoid(gt) for gt in gts]
    cum_cat = jnp.dot(tril1, jnp.concatenate(lfs, axis=1), precision=lax.Precision.HIGHEST,
                      preferred_element_type=f32)
    cums = [cum_cat[:, ci * LANES:(ci + 1) * LANES] for ci in range(cpb)]
    gt_ts = [gt.T for gt in gts]
    cum_ts = [cum.T for cum in cums]

    hd_cols = lambda off, i: slice(off + i * HD, off + (i + 1) * HD)
    heads = range(HEADS)
    for ci in range(cpb):
        rows = slice(ci * CHUNK, (ci + 1) * CHUNK)
        gt, gt_t, cum, cum_t = gts[ci], gt_ts[ci], cums[ci], cum_ts[ci]
        qb = [p_ref[rows, hd_cols(OFF_Q, h)] for h in heads]
        kb = [p_ref[rows, hd_cols(OFF_K, h)] for h in heads]
        vb = [p_ref[rows, hd_cols(OFF_V, h)] for h in heads]
        ig_col = [gt[:, h:h + 1] for h in heads]
        ig_row = [gt_t[h:h + 1, :] for h in heads]
        b_col = [cum[:, HEADS + h:HEADS + h + 1] for h in heads]
        b_row = [cum_t[HEADS + h:HEADS + h + 1, :] for h in heads]
        m_old = [m_ref[h:h + 1, 0:1] for h in heads]
        n_old = [n_ref[h:h + 1, :] for h in heads]
        dmat = [jnp.where(causal, b_col[h] - b_row[h] + ig_row[h], -jnp.inf) for h in heads]
        a_col = [b_col[h] + m_old[h] for h in heads]
        dmax = [jnp.max(dmat[h], axis=-1, keepdims=True) for h in heads]
        qk = [lax.dot_general(qb[h], kb[h], nt, preferred_element_type=f32) for h in heads]
        cq = [lax.dot_general(qb[h], c_ref[h].astype(bf16), nt, preferred_element_type=f32) for h in heads]
        qn = [jnp.sum(qb[h].astype(f32) * n_old[h], axis=-1, keepdims=True) for h in heads]
        mt = [jnp.maximum(a_col[h], dmax[h]) for h in heads]
        w_intra = [jnp.exp(dmat[h] - mt[h]) for h in heads]
        w_inter = [jnp.exp(a_col[h] - mt[h]) for h in heads]
        s = [qk[h] * w_intra[h] for h in heads]
        sv = [jnp.dot(s[h].astype(bf16), vb[h], preferred_element_type=f32) for h in heads]
        ssum = [jnp.sum(s[h], axis=-1, keepdims=True) for h in heads]
        m_new = [mt[h][CHUNK - 1:CHUNK, :] for h in heads]
        b_last = [b_col[h][CHUNK - 1:CHUNK, :] for h in heads]
        g_end = [jnp.exp(b_last[h] - b_col[h] + ig_col[h] - m_new[h]) for h in heads]
        decay = [jnp.exp(b_last[h] + m_old[h] - m_new[h]) for h in heads]
        gv_t = [(g_end[h] * vb[h].astype(f32)).T.astype(bf16) for h in heads]
        kv = [jnp.dot(gv_t[h], kb[h], preferred_element_type=f32) for h in heads]
        gk = [jnp.sum(g_end[h] * kb[h].astype(f32), axis=0, keepdims=True) for h in heads]
        den = [w_inter[h] * qn[h] + ssum[h] for h in heads]
        rden = [1.0 / jnp.maximum(jnp.abs(den[h]), jnp.exp(-mt[h])) for h in heads]
        hml = [(w_inter[h] * cq[h] + sv[h]) * rden[h] for h in heads]
        for h in heads:
            c_ref[h] = decay[h] * c_ref[h] + kv[h]
            n_ref[h:h + 1, :] = decay[h] * n_old[h] + gk[h]
            m_ref[h:h + 1, :] = jnp.broadcast_to(m_new[h], (1, m_ref.shape[1]))
        og = [p_ref[rows, hd_cols(OFF_O, h)].astype(f32) * hml[h] for h in heads]
        ms = [jnp.mean(og[h] * og[h], axis=-1, keepdims=True) for h in heads]
        for h in heads:
            y = (og[h] * lax.rsqrt(ms[h] + EPS) * gmh_ref[layer:layer + 1, hd_cols(0, h)]
                 * p_ref[rows, hd_cols(OFF_ZML, h)].astype(f32))
            merged_s[rows, hd_cols(0, h)] = y.astype(bf16)

        groups = range(GROUPS)
        sp = [jnp.dot(ws_tril[g], p_ref[rows, hd_cols(OFF_VN, g)], preferred_element_type=f32) for g in groups]
        for g in groups:
            y = (p_ref[rows, hd_cols(OFF_U, g)].astype(f32) * (sp[g] + bs_col[g])
                 * p_ref[rows, hd_cols(OFF_ZCM, g)].astype(f32))
            merged_s[rows, hd_cols(D_ML, g)] = y.astype(bf16)

        if ci % OUT_CHUNKS == OUT_CHUNKS - 1:
            project_out(slice((ci + 1 - OUT_CHUNKS) * CHUNK, (ci + 1) * CHUNK))


def _mix(p, g, x2, mod, wout, w_s, b_s, g_mh, gfin, b_ig, b_fg, g_norm, w_in_t, prev, *,
         layer, batch, mod_row_block, cpb, final):
    m = x2.shape[0]
    depth = w_s.shape[0]
    ts = cpb * CHUNK
    steps = m // batch // ts
    tok = lambda b, c: (b * steps + c, 0)
    full2 = lambda b, c: (0, 0)
    smem = pl.BlockSpec(memory_space=pltpu.SMEM)
    mod_spec = lambda lyr, k: pl.BlockSpec((None, None, 8, D_MODEL), lambda b, c: (lyr, k, mod_row_block, 0))
    in_specs = [
        smem, smem,
        pl.BlockSpec((ts, N_MAIN), tok),
        pl.BlockSpec((ts, LANES), tok),
        pl.BlockSpec((ts, D_MODEL), tok),
        mod_spec(layer, 2),
        pl.BlockSpec((None, D_MODEL, D_MODEL), lambda b, c: (layer, 0, 0), pipeline_mode=pl.Buffered(1)),
        pl.BlockSpec((None, GROUPS, CHUNK, CHUNK), lambda b, c: (layer, 0, 0, 0)),
        pl.BlockSpec((None, GROUPS, CHUNK), lambda b, c: (layer, 0, 0)),
        pl.BlockSpec(g_mh.shape, full2),
        pl.BlockSpec((1, D_MODEL), full2),
    ]
    args = [b_ig, b_fg, p, g, x2, mod, wout, w_s, b_s, g_mh, gfin]
    out_specs = [pl.BlockSpec((ts, D_MODEL), tok)]
    out_shape = [jax.ShapeDtypeStruct((m, D_MODEL), f32)]
    if not final:
        in_specs += [
            mod_spec(layer + 1, 0), mod_spec(layer + 1, 1),
            pl.BlockSpec(g_norm.shape, full2),
            _gate_rows_spec(layer + 1),
        ]
        args += [mod, mod, g_norm, w_in_t]
        out_specs += [pl.BlockSpec((ts, D_MODEL), tok), pl.BlockSpec((ts, LANES), tok)]
        out_shape += [jax.ShapeDtypeStruct((m, D_MODEL), bf16), jax.ShapeDtypeStruct((m, LANES), f32)]
    n_state_out = len(out_specs)
    out_specs += [
        pl.BlockSpec((None, None, HEADS, HD, HD), lambda b, c: (layer, b, 0, 0, 0)),
        pl.BlockSpec((None, None, HEADS, HD), lambda b, c: (layer, b, 0, 0)),
        pl.BlockSpec((None, None, 8, LANES), lambda b, c: (layer, b, 0, 0)),
    ]
    out_shape += [
        jax.ShapeDtypeStruct((depth, batch, HEADS, HD, HD), f32),
        jax.ShapeDtypeStruct((depth, batch, HEADS, HD), f32),
        jax.ShapeDtypeStruct((depth, batch, 8, LANES), f32),
    ]
    aliases = {}
    if prev is not None:
        for t, arr in enumerate(prev):
            aliases[len(args)] = n_state_out + t
            in_specs.append(pl.BlockSpec(memory_space=pl.ANY))
            args.append(arr)
    kern = functools.partial(_mix_kernel, layer=layer, cpb=cpb, final=final)
    return pl.pallas_call(
        kern,
        grid=(batch, steps),
        in_specs=in_specs,
        out_specs=out_specs,
        out_shape=out_shape,
        scratch_shapes=[pltpu.VMEM((ts, D_MODEL), bf16)],
        input_output_aliases=aliases,
        compiler_params=_cparams(("arbitrary", "arbitrary")),
        name="mix",
    )(*args)


def _interleave(first, second):
    a, b = list(first), list(second)
    done = 0
    for k, stage in enumerate(a):
        stage()
        want = ((k + 1) * len(b)) // len(a)
        while done < want:
            b[done]()
            done += 1


def _mixp_kernel(bi_ref, bf_ref, p_ref, g_ref, x_ref, gate_ref, wout_ref, ws_ref, bs_ref, gmh_ref, gfin_ref,
                 *rest, layer, cpb, steps_per_b, n_steps, final):
    if final:
        xo_ref, c_ref, n_ref, m_ref = rest[-8:-4]
    else:
        sh_ref, sc_ref, gn_ref, wg_ref = rest[:4]
        xo_ref, hn_ref, gn_out_ref, c_ref, n_ref, m_ref = rest[-10:-4]
    merged_s, c_s, n_s, m_s = rest[-4:]
    t = pl.program_id(0)
    ts = cpb * CHUNK
    slot_a = lax.rem(t, 2)
    slot_b = 1 - slot_a
    b_prev = jnp.maximum(t - 1, 0) // steps_per_b

    @pl.when(t == 0)
    def _():
        merged_s[1] = jnp.zeros((ts, D_MODEL), bf16)

    @pl.when(lax.rem(t, steps_per_b) == 0)
    def _():
        c_s[...] = jnp.zeros_like(c_s)
        n_s[...] = jnp.zeros_like(n_s)
        m_s[...] = jnp.zeros_like(m_s)

    row = lax.broadcasted_iota(jnp.int32, (CHUNK, CHUNK), 0)
    col = lax.broadcasted_iota(jnp.int32, (CHUNK, CHUNK), 1)
    causal = col <= row
    trilb = causal.astype(bf16)
    bias_row = _gate_bias_row(bi_ref, bf_ref, layer)
    nt = NT_DIMS
    hd_cols = lambda off, i: slice(off + i * HD, off + (i + 1) * HD)
    heads = range(HEADS)
    groups = range(GROUPS)
    env = {}

    def a_prologue():
        env["ws"] = [jnp.where(causal, ws_ref[g], 0.0).astype(bf16) for g in groups]
        env["bs"] = [jnp.sum(jnp.where(row == col, bs_ref[g:g + 1, :], 0.0), axis=-1, keepdims=True)
                     for g in groups]
        gts = [g_ref[ci * CHUNK:(ci + 1) * CHUNK, :] + bias_row for ci in range(cpb)]
        lf = jnp.concatenate([_log_sigmoid(gt) for gt in gts], axis=1)
        hi = lf.astype(bf16)
        r1 = lf - hi.astype(f32)
        mid = r1.astype(bf16)
        lo = (r1 - mid.astype(f32)).astype(bf16)
        cum_cat = (jnp.dot(trilb, hi, preferred_element_type=f32) + jnp.dot(trilb, mid, preferred_element_type=f32)
                   + jnp.dot(trilb, lo, preferred_element_type=f32))
        cums = [cum_cat[:, ci * LANES:(ci + 1) * LANES] for ci in range(cpb)]
        env["gt"], env["cum"] = gts, cums
        env["gt_t"] = [gt.T for gt in gts]
        env["cum_t"] = [cum.T for cum in cums]

    def a_chunk(ci):
        rows = slice(ci * CHUNK, (ci + 1) * CHUNK)
        e = {}

        def s1():
            gt, gt_t, cum, cum_t = env["gt"][ci], env["gt_t"][ci], env["cum"][ci], env["cum_t"][ci]
            e["qb"] = [p_ref[rows, hd_cols(OFF_Q, h)] for h in heads]
            e["kb"] = [p_ref[rows, hd_cols(OFF_K, h)] for h in heads]
            e["ig_col"] = [gt[:, h:h + 1] for h in heads]
            ig_row = [gt_t[h:h + 1, :] for h in heads]
            e["b_col"] = [cum[:, HEADS + h:HEADS + h + 1] for h in heads]
            b_row = [cum_t[HEADS + h:HEADS + h + 1, :] for h in heads]
            e["m_old"] = [m_s[h:h + 1, 0:1] for h in heads]
            e["n_old"] = [n_s[h:h + 1, :] for h in heads]
            e["dmat"] = [jnp.where(causal, e["b_col"][h] - b_row[h] + ig_row[h], -jnp.inf) for h in heads]
            e["a_col"] = [e["b_col"][h] + e["m_old"][h] for h in heads]
            e["dmax"] = [jnp.max(e["dmat"][h], axis=-1, keepdims=True) for h in heads]

        def s2():
            e["qk"] = [lax.dot_general(e["qb"][h], e["kb"][h], nt, preferred_element_type=f32) for h in heads]
            e["cq"] = [lax.dot_general(e["qb"][h], c_s[h].astype(bf16), nt, preferred_element_type=f32)
                       for h in heads]
            n_rows = [jnp.broadcast_to(e["n_old"][h].astype(bf16), (16, HD)) for h in heads]
            e["qn"] = [lax.dot_general(e["qb"][h], n_rows[h], nt, preferred_element_type=f32)[:, 0:1]
                       for h in heads]

        def s3():
            e["mt"] = [jnp.maximum(e["a_col"][h], e["dmax"][h]) for h in heads]
            w_intra = [jnp.exp(e["dmat"][h] - e["mt"][h]) for h in heads]
            e["w_inter"] = [jnp.exp(e["a_col"][h] - e["mt"][h]) for h in heads]
            e["s"] = [e["qk"][h] * w_intra[h] for h in heads]
            e["vb"] = [p_ref[rows, hd_cols(OFF_V, h)] for h in heads]

        def s4():
            e["sv"] = [jnp.dot(e["s"][h].astype(bf16), e["vb"][h], preferred_element_type=f32) for h in heads]
            e["ssum"] = [jnp.sum(e["s"][h], axis=-1, keepdims=True) for h in heads]

        def s5():
            e["m_new"] = [e["mt"][h][CHUNK - 1:CHUNK, :] for h in heads]
            b_last = [e["b_col"][h][CHUNK - 1:CHUNK, :] for h in heads]
            e["g_end"] = [jnp.exp(b_last[h] - e["b_col"][h] + e["ig_col"][h] - e["m_new"][h]) for h in heads]
            e["decay"] = [jnp.exp(b_last[h] + e["m_old"][h] - e["m_new"][h]) for h in heads]
            e["gv_t"] = [(e["g_end"][h] * e["vb"][h].astype(f32)).T.astype(bf16) for h in heads]

        def s6():
            kv = [jnp.dot(e["gv_t"][h], e["kb"][h], preferred_element_type=f32) for h in heads]
            gk = [jnp.sum(e["g_end"][h] * e["kb"][h].astype(f32), axis=0, keepdims=True) for h in heads]
            for h in heads:
                c_s[h] = e["decay"][h] * c_s[h] + kv[h]
                n_s[h:h + 1, :] = e["decay"][h] * e["n_old"][h] + gk[h]
                m_s[h:h + 1, :] = jnp.broadcast_to(e["m_new"][h], (1, m_s.shape[1]))

        def s7():
            den = [e["w_inter"][h] * e["qn"][h] + e["ssum"][h] for h in heads]
            rden = [1.0 / jnp.maximum(jnp.abs(den[h]), jnp.exp(-e["mt"][h])) for h in heads]
            hml = [(e["w_inter"][h] * e["cq"][h] + e["sv"][h]) * rden[h] for h in heads]
            e["og"] = [p_ref[rows, hd_cols(OFF_O, h)].astype(f32) * hml[h] for h in heads]
            e["ms"] = [jnp.mean(e["og"][h] * e["og"][h], axis=-1, keepdims=True) for h in heads]

        def s8():
            for h in heads:
                y = (e["og"][h] * lax.rsqrt(e["ms"][h] + EPS) * gmh_ref[layer:layer + 1, hd_cols(0, h)]
                     * p_ref[rows, hd_cols(OFF_ZML, h)].astype(f32))
                merged_s[slot_a, rows, hd_cols(0, h)] = y.astype(bf16)

        def s9():
            e["sp"] = [jnp.dot(env["ws"][g], p_ref[rows, hd_cols(OFF_VN, g)], preferred_element_type=f32)
                       for g in groups]

        def s10():
            for g in groups:
                y = (p_ref[rows, hd_cols(OFF_U, g)].astype(f32) * (e["sp"][g] + env["bs"][g])
                     * p_ref[rows, hd_cols(OFF_ZCM, g)].astype(f32))
                merged_s[slot_a, rows, hd_cols(D_ML, g)] = y.astype(bf16)

        return [s1, s2, s3, s4, s5, s6, s7, s8, s9, s10]

    stages_a = [a_prologue]
    for ci in range(cpb):
        stages_a += a_chunk(ci)

    n_piece = 256
    gate_row = gate_ref[pl.ds(b_prev, 1), :]
    acc_sq = {}

    def b_piece(n):
        def run():
            cols = slice(n * n_piece, (n + 1) * n_piece)
            out = jnp.dot(merged_s[slot_b], wout_ref[:, cols], preferred_element_type=f32)
            xn = x_ref[:, cols] + gate_row[:, cols] * out
            xo_ref[:, cols] = xn
            sq = jnp.sum(xn * xn, axis=-1, keepdims=True)
            acc_sq["v"] = sq if n == 0 else acc_sq["v"] + sq
        return run

    def b_finish(r):
        def run():
            rows = slice(r * CHUNK, (r + 1) * CHUNK)
            scale = lax.rsqrt(acc_sq["v"][rows, :] * (1.0 / D_MODEL) + EPS)
            if final:
                xo_ref[rows, :] = xo_ref[rows, :] * scale * gfin_ref[...]
            else:
                if "gmod" not in acc_sq:
                    acc_sq["gmod"] = gn_ref[layer + 1:layer + 2, :] * (1.0 + sc_ref[pl.ds(b_prev, 1), :])
                    acc_sq["shift"] = sh_ref[pl.ds(b_prev, 1), :]
                hn = (xo_ref[rows, :] * scale * acc_sq["gmod"] + acc_sq["shift"]).astype(bf16)
                hn_ref[rows, :] = hn
                gn_out_ref[rows, :] = lax.dot_general(hn, wg_ref[...].astype(bf16), NT_DIMS,
                                                      preferred_element_type=f32)
        return run

    n_first = 1 + len(stages_a) // cpb
    _interleave(stages_a[:n_first], [b_piece(n) for n in range(D_MODEL // n_piece)])
    _interleave(stages_a[n_first:], [b_finish(r) for r in range(cpb)])

    @pl.when((lax.rem(t, steps_per_b) == steps_per_b - 1) & (t < n_steps))
    def _():
        c_ref[...] = c_s[...]
        n_ref[...] = n_s[...]
        m_ref[...] = m_s[...]


def _mixp(p, g, x2, mod, wout, w_s, b_s, g_mh, gfin, b_ig, b_fg, g_norm, w_in_t, prev, *,
          layer, batch, mod_row_block, cpb, final):
    m = x2.shape[0]
    depth = w_s.shape[0]
    ts = cpb * CHUNK
    n_steps = m // ts
    steps_per_b = n_steps // batch
    cur = lambda t: (jnp.minimum(t, n_steps - 1), 0)
    lag = lambda t: (jnp.maximum(t - 1, 0), 0)
    cur_b = lambda t: jnp.minimum(t, n_steps - 1) // steps_per_b
    full2 = lambda t: (0, 0)
    smem = pl.BlockSpec(memory_space=pltpu.SMEM)
    mod_spec = lambda lyr, k: pl.BlockSpec((None, None, 8, D_MODEL), lambda t: (lyr, k, mod_row_block, 0))
    in_specs = [
        smem, smem,
        pl.BlockSpec((ts, N_MAIN), cur),
        pl.BlockSpec((ts, LANES), cur),
        pl.BlockSpec((ts, D_MODEL), lag),
        mod_spec(layer, 2),
        pl.BlockSpec((None, D_MODEL, D_MODEL), lambda t: (layer, 0, 0), pipeline_mode=pl.Buffered(1)),
        pl.BlockSpec((None, GROUPS, CHUNK, CHUNK), lambda t: (layer, 0, 0, 0)),
        pl.BlockSpec((None, GROUPS, CHUNK), lambda t: (layer, 0, 0)),
        pl.BlockSpec(g_mh.shape, full2),
        pl.BlockSpec((1, D_MODEL), full2),
    ]
    args = [b_ig, b_fg, p, g, x2, mod, wout, w_s, b_s, g_mh, gfin]
    out_specs = [pl.BlockSpec((ts, D_MODEL), lag)]
    out_shape = [jax.ShapeDtypeStruct((m, D_MODEL), f32)]
    if not final:
        in_specs += [
            mod_spec(layer + 1, 0), mod_spec(layer + 1, 1),
            pl.BlockSpec(g_norm.shape, full2),
            _gate_rows_spec(layer + 1),
        ]
        args += [mod, mod, g_norm, w_in_t]
        out_specs += [pl.BlockSpec((ts, D_MODEL), lag), pl.BlockSpec((ts, LANES), lag)]
        out_shape += [jax.ShapeDtypeStruct((m, D_MODEL), bf16), jax.ShapeDtypeStruct((m, LANES), f32)]
    n_state_out = len(out_specs)
    out_specs += [
        pl.BlockSpec((None, None, HEADS, HD, HD), lambda t: (layer, cur_b(t), 0, 0, 0)),
        pl.BlockSpec((None, None, HEADS, HD), lambda t: (layer, cur_b(t), 0, 0)),
        pl.BlockSpec((None, None, 8, LANES), lambda t: (layer, cur_b(t), 0, 0)),
    ]
    out_shape += [
        jax.ShapeDtypeStruct((depth, batch, HEADS, HD, HD), f32),
        jax.ShapeDtypeStruct((depth, batch, HEADS, HD), f32),
        jax.ShapeDtypeStruct((depth, batch, 8, LANES), f32),
    ]
    aliases = {}
    if prev is not None:
        for k, arr in enumerate(prev):
            aliases[len(args)] = n_state_out + k
            in_specs.append(pl.BlockSpec(memory_space=pl.ANY))
            args.append(arr)
    kern = functools.partial(_mixp_kernel, layer=layer, cpb=cpb, steps_per_b=steps_per_b, n_steps=n_steps,
                             final=final)
    return pl.pallas_call(
        kern,
        grid=(n_steps + 1,),
        in_specs=in_specs,
        out_specs=out_specs,
        out_shape=out_shape,
        scratch_shapes=[
            pltpu.VMEM((2, ts, D_MODEL), bf16),
            pltpu.VMEM((HEADS, HD, HD), f32),
            pltpu.VMEM((HEADS, HD), f32),
            pltpu.VMEM((8, LANES), f32),
        ],
        input_output_aliases=aliases,
        compiler_params=_cparams(("arbitrary",)),
        name="mix",
    )(*args)


def _spre_kernel(bi_ref, bf_ref, p_ref, g_ref, n_ref, m_ref,
                 gk_ref, nn_ref, mn_ref, dec_ref, s_ref, stab_ref, vt_ref, *, layer):
    gt = g_ref[...] + _gate_bias_row(bi_ref, bf_ref, layer)
    lf = _log_sigmoid(gt)
    for h in range(HEADS):
        hs = slice(h * HD, (h + 1) * HD)
        q = p_ref[:, OFF_Q + h * HD:OFF_Q + (h + 1) * HD]
        k = p_ref[:, OFF_K + h * HD:OFF_K + (h + 1) * HD]
        ig = gt[:, h:h + 1]
        a = lf[:, HEADS + h:HEADS + h + 1] + m_ref[:, h:h + 1]
        mt = jnp.maximum(a, ig)
        w_intra = jnp.exp(ig - mt)
        w_inter = jnp.exp(a - mt)
        n_old = n_ref[:, hs]
        s = jnp.sum(q * k, axis=-1, keepdims=True) * w_intra
        den = w_inter * jnp.sum(n_old * q, axis=-1, keepdims=True) + s
        gk = w_intra * k
        gk_ref[:, hs] = gk
        nn_ref[:, hs] = w_inter * n_old + gk
        mn_ref[:, h:h + 1] = mt
        dec_ref[:, h:h + 1] = w_inter
        s_ref[:, h:h + 1] = s
        stab_ref[:, h:h + 1] = jnp.maximum(jnp.abs(den), jnp.exp(-mt))
    for t in range(D_ML // LANES):
        vt_ref[t * LANES:(t + 1) * LANES, :] = p_ref[:, OFF_V + t * LANES:OFF_V + (t + 1) * LANES].T


def _spre(p, g, b_ig, b_fg, state_n, state_m, *, layer):
    b = p.shape[0]
    full = lambda i: (0, 0)
    smem = pl.BlockSpec(memory_space=pltpu.SMEM)
    small = jax.ShapeDtypeStruct((b, HEADS), f32)
    wide = jax.ShapeDtypeStruct((b, D_ML), f32)
    small_spec = pl.BlockSpec((b, HEADS), full)
    wide_spec = pl.BlockSpec((b, D_ML), full)
    return pl.pallas_call(
        functools.partial(_spre_kernel, layer=layer),
        grid=(1,),
        in_specs=[
            smem, smem,
            pl.BlockSpec((b, 3 * D_ML), full),
            pl.BlockSpec((b, LANES), full),
            pl.BlockSpec((None, b, D_ML), lambda i: (layer, 0, 0)),
            pl.BlockSpec((None, b, HEADS), lambda i: (layer, 0, 0)),
        ],
        out_specs=[wide_spec, wide_spec, small_spec, small_spec, small_spec, small_spec,
                   pl.BlockSpec((D_ML, b), full)],
        out_shape=[wide, wide, small, small, small, small, jax.ShapeDtypeStruct((D_ML, b), f32)],
        compiler_params=_cparams(("arbitrary",)),
        name="sample_pre",
    )(b_ig, b_fg, p, g, state_n, state_m)


def _sc_kernel(dec_ref, c_ref, q_ref, gk_ref, vt_ref, *rest, bb):
    co_ref, cqt_ref = rest[-2:]
    j = pl.program_id(0)

    @pl.when(j == 0)
    def _():
        cqt_ref[...] = jnp.zeros_like(cqt_ref)

    lane = lax.broadcasted_iota(jnp.int32, (HD, cqt_ref.shape[1]), 1)
    for i in range(bb):
        b = j * bb + i
        onehot = (lane == b).astype(f32)
        for h in range(HEADS):
            hs = slice(h * HD, (h + 1) * HD)
            c_old = c_ref[i, h]
            vcol = jnp.sum(vt_ref[hs, :] * onehot, axis=-1, keepdims=True)
            cq = jnp.sum(c_old * q_ref[i:i + 1, hs], axis=-1, keepdims=True)
            cqt_ref[hs, :] += cq * onehot
            co_ref[i, h] = dec_ref[b * HEADS + h] * c_old + vcol * gk_ref[i:i + 1, hs]


def _sc(dec_flat, state_c, layer, p, gk, vt, c_out_prev, *, bb):
    b = p.shape[0]
    in_specs = [
        pl.BlockSpec(memory_space=pltpu.SMEM),
        pl.BlockSpec((None, bb, HEADS, HD, HD), lambda j: (layer, j, 0, 0, 0)),
        pl.BlockSpec((bb, D_ML), lambda j: (j, 0)),
        pl.BlockSpec((bb, D_ML), lambda j: (j, 0)),
        pl.BlockSpec((D_ML, b), lambda j: (0, 0)),
    ]
    args = [dec_flat, state_c, p, gk, vt]
    aliases = {}
    if c_out_prev is not None:
        in_specs.append(pl.BlockSpec(memory_space=pl.ANY))
        args.append(c_out_prev)
        aliases = {5: 0}
    return pl.pallas_call(
        functools.partial(_sc_kernel, bb=bb),
        grid=(b // bb,),
        in_specs=in_specs,
        out_specs=[
            pl.BlockSpec((None, bb, HEADS, HD, HD), lambda j: (layer, j, 0, 0, 0)),
            pl.BlockSpec((D_ML, b), lambda j: (0, 0)),
        ],
        out_shape=[
            jax.ShapeDtypeStruct(state_c.shape, f32),
            jax.ShapeDtypeStruct((D_ML, b), f32),
        ],
        input_output_aliases=aliases,
        compiler_params=_cparams(("arbitrary",)),
        name="sample_state",
    )(*args)


def _spost_kernel(ws_ref, bs_ref, cqt_ref, p_ref, dec_ref, s_ref, stab_ref, x_ref, gate_ref, wout_ref,
                  gmh_ref, gfin_ref, xo_ref, merged_s, *, layer, final):
    for h in range(HEADS):
        hs = slice(h * HD, (h + 1) * HD)
        cq = jnp.concatenate([cqt_ref[h * HD + t * LANES:h * HD + (t + 1) * LANES, :].T
                              for t in range(HD // LANES)], axis=1)
        v = p_ref[:, OFF_V + h * HD:OFF_V + (h + 1) * HD]
        o_act = p_ref[:, OFF_O + h * HD:OFF_O + (h + 1) * HD]
        z_act = p_ref[:, OFF_ZML + h * HD:OFF_ZML + (h + 1) * HD]
        num = dec_ref[:, h:h + 1] * cq + s_ref[:, h:h + 1] * v
        hml = num / stab_ref[:, h:h + 1]
        y = _rms(o_act * hml, gmh_ref[layer:layer + 1, hs]) * z_act
        merged_s[:, hs] = y.astype(bf16)
    for g in range(GROUPS):
        u_act = p_ref[:, OFF_U + g * HD:OFF_U + (g + 1) * HD]
        vn = p_ref[:, OFF_VN + g * HD:OFF_VN + (g + 1) * HD]
        z_act = p_ref[:, OFF_ZCM + g * HD:OFF_ZCM + (g + 1) * HD]
        sp = ws_ref[layer, g] * vn + bs_ref[layer, g]
        merged_s[:, D_ML + g * HD:D_ML + (g + 1) * HD] = (u_act * sp * z_act).astype(bf16)
    out = jnp.dot(merged_s[...], wout_ref[...], preferred_element_type=f32)
    xn = x_ref[...] + gate_ref[...] * out
    if final:
        xn = _rms(xn, gfin_ref[...])
    xo_ref[...] = xn


def _spost(ws0, bs0, cqt, p, dec, s, stab, x2, mod, wout, g_mh, gfin, *, layer, final):
    b = p.shape[0]
    full = lambda i: (0, 0)
    smem = pl.BlockSpec(memory_space=pltpu.SMEM)
    small = pl.BlockSpec((b, HEADS), full)
    return pl.pallas_call(
        functools.partial(_spost_kernel, layer=layer, final=final),
        grid=(1,),
        in_specs=[
            smem, smem,
            pl.BlockSpec((D_ML, b), full),
            pl.BlockSpec((b, N_MAIN), full),
            small, small, small,
            pl.BlockSpec((b, D_MODEL), full),
            pl.BlockSpec((None, None, b, D_MODEL), lambda i: (layer, 2, MOD_SAMPLE_ROW, 0)),
            pl.BlockSpec((None, D_MODEL, D_MODEL), lambda i: (layer, 0, 0)),
            pl.BlockSpec(g_mh.shape, full),
            pl.BlockSpec((1, D_MODEL), full),
        ],
        out_specs=pl.BlockSpec((b, D_MODEL), full),
        out_shape=jax.ShapeDtypeStruct((b, D_MODEL), f32),
        scratch_shapes=[pltpu.VMEM((b, D_MODEL), bf16)],
        compiler_params=_cparams(("arbitrary",)),
        name="sample_post",
    )(ws0, bs0, cqt, p, dec, s, stab, x2, mod, wout, g_mh, gfin)


def kernel(x_prompt, x_sample, state_C, state_n, state_m, c_prompt, c_sample, g_norm, w_ada, b_ada, w_in,
           b_igate, b_fgate, g_mh, g_cmv, w_s, b_s, w_out, g_final):
    depth = w_in.shape[0]
    bp, seq, _ = x_prompt.shape
    bs = x_sample.shape[0]
    assert bs % 8 == 0 and bp <= 8

    c_all = jnp.concatenate([c_sample, c_prompt, jnp.zeros((8 - bp, D_MODEL), f32)], axis=0)
    mod = _ada(c_all, w_ada, b_ada)
    prompt_row_block = bs // 8
    wout = _cast_bf16(w_out)

    xp = x_prompt.reshape(bp * seq, D_MODEL)
    xs = x_sample.reshape(bs, D_MODEL)
    gfin = g_final.reshape(1, D_MODEL)
    state_n2 = state_n.reshape(depth, bs, D_ML)
    ws0 = w_s[:, :, 0, 0]
    bs0 = b_s[:, :, 0]
    w_in = jnp.swapaxes(w_in, 1, 2)

    hp, gp = _prenorm(xp, mod, g_norm, w_in, layer=0, tm=512, mod_row_block=prompt_row_block, mod_rows=8,
                      per_row_mod=False, rows_per_b=seq)
    prev = None
    cs_all = None
    ns, ms, vs = [], [], []
    for l in range(depth):
        final = l == depth - 1
        p = _in_proj(hp, w_in, g_cmv, layer=l, tm=1024, out_dtype=bf16)
        res = _mixp(p, gp, xp, mod, wout, w_s, b_s, g_mh, gfin, b_igate, b_fgate, g_norm, w_in, prev,
                    layer=l, batch=bp, mod_row_block=prompt_row_block, cpb=2, final=final)
        if final:
            xp, *prev = res
        else:
            xp, hp, gp, *prev = res

        hs_, gs_ = _prenorm(xs, mod, g_norm, w_in, layer=l, tm=bs, mod_row_block=MOD_SAMPLE_ROW, mod_rows=bs,
                            per_row_mod=True, rows_per_b=None)
        p = _in_proj(hs_, w_in, g_cmv, layer=l, tm=bs, out_dtype=f32)
        gk, nn, mn, dec, s, stab, vt = _spre(p, gs_, b_igate, b_fgate, state_n2, state_m, layer=l)
        cs_all, cqt = _sc(dec.reshape(bs * HEADS), state_C, l, p, gk, vt, cs_all, bb=8)
        xs = _spost(ws0, bs0, cqt, p, dec, s, stab, xs, mod, wout, g_mh, gfin, layer=l, final=final)
        ns.append(nn.reshape(bs, HEADS, HD))
        ms.append(mn)
        vs.append(p[:, OFF_VN:OFF_VN + D_CM].reshape(bs, 1, D_CM))

    cp_all, np_all, mp_all = prev
    return (xp.reshape(bp, seq, D_MODEL), xs.reshape(bs, 1, D_MODEL),
            cp_all, np_all, mp_all[:, :, :HEADS, 0],
            cs_all, jnp.stack(ns), jnp.stack(ms), jnp.stack(vs))
```

```python
import functools

import jax
import jax.numpy as jnp
from jax import lax
from jax.experimental import pallas as pl
from jax.experimental.pallas import tpu as pltpu

f32 = jnp.float32
bf16 = jnp.bfloat16

D_MODEL = 2048
D_ML = 1024
D_CM = 1024
HEADS = 4
HD = 256
GROUPS = 4
CHUNK = 128
LANES = 128
KIND_W = 1024
N_KINDS = 8
N_MAIN = N_KINDS * KIND_W
N_GATES = 2 * HEADS
GATE_KIND = 5
EPS = 1e-6
OFF_Q, OFF_K, OFF_V, OFF_O, OFF_ZML, OFF_U, OFF_VN, OFF_ZCM = (i * KIND_W for i in range(N_KINDS))
MOD_SAMPLE_ROW = 0
VMEM_LIMIT = 56 * 1024 * 1024
NT_DIMS = (((1,), (1,)), ((), ()))


def _cparams(sem):
    return pltpu.CompilerParams(dimension_semantics=sem, vmem_limit_bytes=VMEM_LIMIT)


def _gate_rows_spec(layer):
    blk = GATE_KIND * KIND_W // LANES
    return pl.BlockSpec((None, LANES, D_MODEL), lambda *ids: (layer, blk, 0))


def _sigmoid(x):
    return 1.0 / (1.0 + jnp.exp(-x))


def _silu(x):
    return x * _sigmoid(x)


def _gelu(x):
    return 0.5 * x * (1.0 + jnp.tanh(0.7978845608028654 * (x + 0.044715 * (x * x * x))))


def _log_sigmoid(x):
    return jnp.minimum(x, 0.0) - jnp.log1p(jnp.exp(-jnp.abs(x)))


def _rms(x, g):
    return x * lax.rsqrt(jnp.mean(x * x, axis=-1, keepdims=True) + EPS) * g


def _layernorm(x, g):
    mu = jnp.mean(x, axis=-1, keepdims=True)
    xc = x - mu
    return xc * lax.rsqrt(jnp.mean(xc * xc, axis=-1, keepdims=True) + EPS) * g


def _gate_bias_row(bi_ref, bf_ref, layer):
    lane = lax.broadcasted_iota(jnp.int32, (1, LANES), 1)
    row = jnp.zeros((1, LANES), f32)
    for h in range(HEADS):
        row = jnp.where(lane == h, bi_ref[layer, h], row)
        row = jnp.where(lane == HEADS + h, bf_ref[layer, h], row)
    return row


def _modulated_norm(x, gn, sc, sh):
    return _rms(x, gn) * (1.0 + sc) + sh


def _interleave(first, second):
    done = 0
    for k, stage in enumerate(first):
        stage()
        want = ((k + 1) * len(second)) // len(first)
        while done < want:
            second[done]()
            done += 1


def _ada_kernel(c_ref, w_ref, b_ref, o_ref):
    a = _silu(c_ref[...]).astype(bf16)
    o_ref[...] = jnp.dot(a, w_ref[...].astype(bf16), preferred_element_type=f32) + b_ref[...]


def _ada(c_all, w_ada, b_ada):
    depth, _, n3 = w_ada.shape
    rows = c_all.shape[0]
    tn = 512
    per = D_MODEL // tn
    return pl.pallas_call(
        _ada_kernel,
        grid=(depth, n3 // tn),
        in_specs=[
            pl.BlockSpec((rows, D_MODEL), lambda l, j: (0, 0)),
            pl.BlockSpec((None, D_MODEL, tn), lambda l, j: (l, 0, j)),
            pl.BlockSpec((None, 1, tn), lambda l, j: (l, 0, j)),
        ],
        out_specs=pl.BlockSpec((None, None, rows, tn), lambda l, j: (l, j // per, 0, j % per)),
        out_shape=jax.ShapeDtypeStruct((depth, 3, rows, D_MODEL), f32),
        compiler_params=_cparams(("arbitrary", "arbitrary")),
        name="ada",
    )(c_all, w_ada, b_ada.reshape(depth, 1, n3))


def _cast_kernel(w_ref, o_ref):
    o_ref[...] = w_ref[...].astype(bf16)


def _cast_bf16(w):
    depth, r, c = w.shape
    tr = 512
    return pl.pallas_call(
        _cast_kernel,
        grid=(depth, r // tr),
        in_specs=[pl.BlockSpec((None, tr, c), lambda l, i: (l, i, 0))],
        out_specs=pl.BlockSpec((None, tr, c), lambda l, i: (l, i, 0)),
        out_shape=jax.ShapeDtypeStruct(w.shape, bf16),
        compiler_params=_cparams(("arbitrary", "arbitrary")),
        name="cast_w_out",
    )(w)


def _prenorm_kernel(x_ref, sh_ref, sc_ref, gn_ref, wg_ref, h_ref, g_ref, *, layer, per_row_mod, rows_per_b):
    if per_row_mod:
        sh, sc = sh_ref[...], sc_ref[...]
    else:
        b = (pl.program_id(0) * x_ref.shape[0]) // rows_per_b
        sh, sc = sh_ref[pl.ds(b, 1), :], sc_ref[pl.ds(b, 1), :]
    h = _modulated_norm(x_ref[...], gn_ref[layer:layer + 1, :], sc, sh).astype(bf16)
    h_ref[...] = h
    g_ref[...] = lax.dot_general(h, wg_ref[...].astype(bf16), NT_DIMS, preferred_element_type=f32)


def _prenorm(x2, mod, g_norm, w_in_t, *, layer, tm, mod_row_block, mod_rows, per_row_mod, rows_per_b):
    m = x2.shape[0]
    mod_spec = lambda k: pl.BlockSpec((None, None, mod_rows, D_MODEL), lambda i: (layer, k, mod_row_block, 0))
    return pl.pallas_call(
        functools.partial(_prenorm_kernel, layer=layer, per_row_mod=per_row_mod, rows_per_b=rows_per_b),
        grid=(m // tm,),
        in_specs=[
            pl.BlockSpec((tm, D_MODEL), lambda i: (i, 0)),
            mod_spec(0),
            mod_spec(1),
            pl.BlockSpec(g_norm.shape, lambda i: (0, 0)),
            _gate_rows_spec(layer),
        ],
        out_specs=[pl.BlockSpec((tm, D_MODEL), lambda i: (i, 0)), pl.BlockSpec((tm, LANES), lambda i: (i, 0))],
        out_shape=[jax.ShapeDtypeStruct((m, D_MODEL), bf16), jax.ShapeDtypeStruct((m, LANES), f32)],
        compiler_params=_cparams(("arbitrary",)),
        name="prenorm",
    )(x2, mod, mod, g_norm, w_in_t)


def _in_kernel(h_ref, wm_ref, wx_ref, gcmv_ref, *rest, layer, tm, rc, n_i, side_bb):
    if side_bb:
        dec_ref, c_ref, q_ref, gk_ref, vt_ref = rest[:5]
        p_ref, co_ref, cqt_ref, wb_s = rest[-4:]
    else:
        p_ref, wb_s = rest[-2:]
    j = pl.program_id(0)
    i = pl.program_id(1)
    kc = 256
    nk = KIND_W // kc

    @pl.when(i == 0)
    def _():
        @pl.when(j < GATE_KIND)
        def _():
            for r in range(nk):
                wb_s[r * kc:(r + 1) * kc, :] = wm_ref[r * kc:(r + 1) * kc, :].astype(bf16)

        @pl.when(j >= GATE_KIND)
        def _():
            for r in range(nk - 1):
                wb_s[r * kc:(r + 1) * kc, :] = wm_ref[r * kc + N_GATES:(r + 1) * kc + N_GATES, :].astype(bf16)
            last = jnp.concatenate([wm_ref[(nk - 1) * kc + N_GATES:, :], wx_ref[...]], axis=0)
            wb_s[(nk - 1) * kc:, :] = last.astype(bf16)

    if side_bb:
        step = j * n_i + i

        @pl.when(step == 0)
        def _():
            cqt_ref[...] = jnp.zeros_like(cqt_ref)

        lane = lax.broadcasted_iota(jnp.int32, (HD, cqt_ref.shape[1]), 1)

        def side_unit(u):
            ib, h = divmod(u, HEADS)
            b = step * side_bb + ib
            hs = slice(h * HD, (h + 1) * HD)
            onehot = (lane == b).astype(f32)
            c_old = c_ref[ib, h]
            vcol = jnp.sum(vt_ref[hs, :] * onehot, axis=-1, keepdims=True)
            cq = jnp.sum(c_old * q_ref[pl.ds(b, 1), hs], axis=-1, keepdims=True)
            cqt_ref[hs, :] += cq * onehot
            co_ref[ib, h] = dec_ref[b * HEADS + h] * c_old + vcol * gk_ref[pl.ds(b, 1), hs]

        side = [functools.partial(side_unit, u) for u in range(side_bb * HEADS)]
    else:
        side = []

    def vn_act(acc):
        g = gcmv_ref[layer:layer + 1, :]
        return jnp.concatenate([_layernorm(_gelu(acc[:, t * HD:(t + 1) * HD]), g[:, t * HD:(t + 1) * HD])
                                for t in range(GROUPS)], axis=1)

    def run(act):
        def chunk(r):
            acc = lax.dot_general(h_ref[r * rc:(r + 1) * rc, :], wb_s[...], NT_DIMS, preferred_element_type=f32)
            p_ref[r * rc:(r + 1) * rc, :] = act(acc).astype(p_ref.dtype)

        _interleave([functools.partial(chunk, r) for r in range(tm // rc)], side)

    pl.when((j == 0) | (j == 2))(lambda: run(lambda a: a))
    pl.when(j == 1)(lambda: run(lambda a: a * (HD ** -0.5)))
    pl.when(j == 3)(lambda: run(_sigmoid))
    pl.when((j == 4) | (j == 7))(lambda: run(_silu))
    pl.when(j == 5)(lambda: run(_gelu))
    pl.when(j == 6)(lambda: run(vn_act))


def _in_proj(h, w_in_t, g_cmv, *, layer, tm, out_dtype, side=None):
    m = h.shape[0]
    rc = min(tm, 256)
    n_i = m // tm
    in_specs = [
        pl.BlockSpec((tm, D_MODEL), lambda j, i: (i, 0)),
        pl.BlockSpec((None, KIND_W, D_MODEL), lambda j, i: (layer, j, 0)),
        pl.BlockSpec((None, N_GATES, D_MODEL), lambda j, i: (layer, (j + 1) * (KIND_W // N_GATES), 0)),
        pl.BlockSpec(g_cmv.shape, lambda j, i: (0, 0)),
    ]
    args = [h, w_in_t, w_in_t, g_cmv]
    out_specs = [pl.BlockSpec((tm, KIND_W), lambda j, i: (i, j))]
    out_shape = [jax.ShapeDtypeStruct((m, N_MAIN), out_dtype)]
    aliases = {}
    side_bb = 0
    if side is not None:
        dec_flat, state_c, p_s, gk, vt, c_out_prev = side
        bs = p_s.shape[0]
        side_bb = bs // (N_KINDS * n_i)
        assert side_bb * N_KINDS * n_i == bs
        blk = lambda j, i: (layer, j * n_i + i, 0, 0, 0)
        in_specs += [
            pl.BlockSpec(memory_space=pltpu.SMEM),
            pl.BlockSpec((None, side_bb, HEADS, HD, HD), blk),
            pl.BlockSpec((bs, D_ML), lambda j, i: (0, 0)),
            pl.BlockSpec((bs, D_ML), lambda j, i: (0, 0)),
            pl.BlockSpec((D_ML, bs), lambda j, i: (0, 0)),
        ]
        args += [dec_flat, state_c, p_s, gk, vt]
        out_specs += [pl.BlockSpec((None, side_bb, HEADS, HD, HD), blk),
                      pl.BlockSpec((D_ML, bs), lambda j, i: (0, 0))]
        out_shape += [jax.ShapeDtypeStruct(state_c.shape, f32), jax.ShapeDtypeStruct((D_ML, bs), f32)]
        if c_out_prev is not None:
            aliases[len(args)] = 1
            in_specs.append(pl.BlockSpec(memory_space=pl.ANY))
            args.append(c_out_prev)
    res = pl.pallas_call(
        functools.partial(_in_kernel, layer=layer, tm=tm, rc=rc, n_i=n_i, side_bb=side_bb),
        grid=(N_KINDS, n_i),
        in_specs=in_specs,
        out_specs=out_specs,
        out_shape=out_shape,
        scratch_shapes=[pltpu.VMEM((KIND_W, D_MODEL), bf16)],
        input_output_aliases=aliases,
        compiler_params=_cparams(("arbitrary", "arbitrary")),
        name="in_proj",
    )(*args)
    return res if side is not None else res[0]


def _mix_kernel(bi_ref, bf_ref, p_ref, g_ref, x_ref, gate_ref, wout_ref, ws_ref, bs_ref, gmh_ref, gfin_ref,
                *rest, layer, cpb, steps_per_b, n_steps, final):
    if final:
        xo_ref, c_ref, n_ref, m_ref = rest[-8:-4]
    else:
        sh_ref, sc_ref, gn_ref, wg_ref = rest[:4]
        xo_ref, hn_ref, gn_out_ref, c_ref, n_ref, m_ref = rest[-10:-4]
    merged_s, c_s, n_s, m_s = rest[-4:]
    t = pl.program_id(0)
    ts = cpb * CHUNK
    slot_a = lax.rem(t, 2)
    slot_b = 1 - slot_a
    b_prev = jnp.maximum(t - 1, 0) // steps_per_b

    @pl.when(t == 0)
    def _():
        merged_s[1] = jnp.zeros((ts, D_MODEL), bf16)

    @pl.when(lax.rem(t, steps_per_b) == 0)
    def _():
        c_s[...] = jnp.zeros_like(c_s)
        n_s[...] = jnp.zeros_like(n_s)
        m_s[...] = jnp.zeros_like(m_s)

    row = lax.broadcasted_iota(jnp.int32, (CHUNK, CHUNK), 0)
    col = lax.broadcasted_iota(jnp.int32, (CHUNK, CHUNK), 1)
    causal = col <= row
    trilb = causal.astype(bf16)
    bias_row = _gate_bias_row(bi_ref, bf_ref, layer)
    nt = NT_DIMS
    hd_cols = lambda off, i: slice(off + i * HD, off + (i + 1) * HD)
    heads = range(HEADS)
    groups = range(GROUPS)
    env = {}

    def a_prologue():
        env["ws"] = [jnp.where(causal, ws_ref[g], 0.0).astype(bf16) for g in groups]
        env["bs"] = [jnp.sum(jnp.where(row == col, bs_ref[g:g + 1, :], 0.0), axis=-1, keepdims=True)
                     for g in groups]
        gts = [g_ref[ci * CHUNK:(ci + 1) * CHUNK, :] + bias_row for ci in range(cpb)]
        lf = jnp.concatenate([_log_sigmoid(gt) for gt in gts], axis=1)
        hi = lf.astype(bf16)
        r1 = lf - hi.astype(f32)
        mid = r1.astype(bf16)
        lo = (r1 - mid.astype(f32)).astype(bf16)
        cum_cat = (jnp.dot(trilb, hi, preferred_element_type=f32) + jnp.dot(trilb, mid, preferred_element_type=f32)
                   + jnp.dot(trilb, lo, preferred_element_type=f32))
        cums = [cum_cat[:, ci * LANES:(ci + 1) * LANES] for ci in range(cpb)]
        env["gt"], env["cum"] = gts, cums
        env["gt_t"] = [gt.T for gt in gts]
        env["cum_t"] = [cum.T for cum in cums]

    def a_chunk(ci):
        rows = slice(ci * CHUNK, (ci + 1) * CHUNK)
        e = {}

        def s1():
            gt, gt_t, cum, cum_t = env["gt"][ci], env["gt_t"][ci], env["cum"][ci], env["cum_t"][ci]
            e["qb"] = [p_ref[rows, hd_cols(OFF_Q, h)] for h in heads]
            e["kb"] = [p_ref[rows, hd_cols(OFF_K, h)] for h in heads]
            e["ig_col"] = [gt[:, h:h + 1] for h in heads]
            ig_row = [gt_t[h:h + 1, :] for h in heads]
            e["b_col"] = [cum[:, HEADS + h:HEADS + h + 1] for h in heads]
            b_row = [cum_t[HEADS + h:HEADS + h + 1, :] for h in heads]
            e["m_old"] = [m_s[h:h + 1, 0:1] for h in heads]
            e["n_old"] = [n_s[h:h + 1, :] for h in heads]
            e["dmat"] = [jnp.where(causal, e["b_col"][h] - b_row[h] + ig_row[h], -jnp.inf) for h in heads]
            e["a_col"] = [e["b_col"][h] + e["m_old"][h] for h in heads]
            e["dmax"] = [jnp.max(e["dmat"][h], axis=-1, keepdims=True) for h in heads]

        def s2():
            e["qk"] = [lax.dot_general(e["qb"][h], e["kb"][h], nt, preferred_element_type=f32) for h in heads]
            e["cq"] = [lax.dot_general(e["qb"][h], c_s[h].astype(bf16), nt, preferred_element_type=f32)
                       for h in heads]
            n_rows = [jnp.broadcast_to(e["n_old"][h].astype(bf16), (16, HD)) for h in heads]
            e["qn"] = [lax.dot_general(e["qb"][h], n_rows[h], nt, preferred_element_type=f32)[:, 0:1]
                       for h in heads]

        def s3():
            e["mt"] = [jnp.maximum(e["a_col"][h], e["dmax"][h]) for h in heads]
            w_intra = [jnp.exp(e["dmat"][h] - e["mt"][h]) for h in heads]
            e["w_inter"] = [jnp.exp(e["a_col"][h] - e["mt"][h]) for h in heads]
            e["s"] = [e["qk"][h] * w_intra[h] for h in heads]
            e["vb"] = [p_ref[rows, hd_cols(OFF_V, h)] for h in heads]

        def s4():
            e["sv"] = [jnp.dot(e["s"][h].astype(bf16), e["vb"][h], preferred_element_type=f32) for h in heads]
            e["ssum"] = [jnp.sum(e["s"][h], axis=-1, keepdims=True) for h in heads]

        def s5():
            e["m_new"] = [e["mt"][h][CHUNK - 1:CHUNK, :] for h in heads]
            b_last = [e["b_col"][h][CHUNK - 1:CHUNK, :] for h in heads]
            e["g_end"] = [jnp.exp(b_last[h] - e["b_col"][h] + e["ig_col"][h] - e["m_new"][h]) for h in heads]
            e["decay"] = [jnp.exp(b_last[h] + e["m_old"][h] - e["m_new"][h]) for h in heads]
            e["gv_t"] = [(e["g_end"][h] * e["vb"][h].astype(f32)).T.astype(bf16) for h in heads]

        def s6():
            kv = [jnp.dot(e["gv_t"][h], e["kb"][h], preferred_element_type=f32) for h in heads]
            gk = [jnp.sum(e["g_end"][h] * e["kb"][h].astype(f32), axis=0, keepdims=True) for h in heads]
            for h in heads:
                c_s[h] = e["decay"][h] * c_s[h] + kv[h]
                n_s[h:h + 1, :] = e["decay"][h] * e["n_old"][h] + gk[h]
                m_s[h:h + 1, :] = jnp.broadcast_to(e["m_new"][h], (1, m_s.shape[1]))

        def s7():
            den = [e["w_inter"][h] * e["qn"][h] + e["ssum"][h] for h in heads]
            rden = [1.0 / jnp.maximum(jnp.abs(den[h]), jnp.exp(-e["mt"][h])) for h in heads]
            hml = [(e["w_inter"][h] * e["cq"][h] + e["sv"][h]) * rden[h] for h in heads]
            e["og"] = [p_ref[rows, hd_cols(OFF_O, h)].astype(f32) * hml[h] for h in heads]
            e["ms"] = [jnp.mean(e["og"][h] * e["og"][h], axis=-1, keepdims=True) for h in heads]

        def s8():
            for h in heads:
                y = (e["og"][h] * lax.rsqrt(e["ms"][h] + EPS) * gmh_ref[layer:layer + 1, hd_cols(0, h)]
                     * p_ref[rows, hd_cols(OFF_ZML, h)].astype(f32))
                merged_s[slot_a, rows, hd_cols(0, h)] = y.astype(bf16)

        def s9():
            e["sp"] = [jnp.dot(env["ws"][g], p_ref[rows, hd_cols(OFF_VN, g)], preferred_element_type=f32)
                       for g in groups]

        def s10():
            for g in groups:
                y = (p_ref[rows, hd_cols(OFF_U, g)].astype(f32) * (e["sp"][g] + env["bs"][g])
                     * p_ref[rows, hd_cols(OFF_ZCM, g)].astype(f32))
                merged_s[slot_a, rows, hd_cols(D_ML, g)] = y.astype(bf16)

        return [s1, s2, s3, s4, s5, s6, s7, s8, s9, s10]

    stages_a = [a_prologue]
    for ci in range(cpb):
        stages_a += a_chunk(ci)

    n_piece = 256
    gate_row = gate_ref[pl.ds(b_prev, 1), :]
    acc = {}

    def b_piece(n):
        def run():
            cols = slice(n * n_piece, (n + 1) * n_piece)
            out = jnp.dot(merged_s[slot_b], wout_ref[:, cols], preferred_element_type=f32)
            xn = x_ref[:, cols] + gate_row[:, cols] * out
            xo_ref[:, cols] = xn
            sq = jnp.sum(xn * xn, axis=-1, keepdims=True)
            acc["sq"] = sq if n == 0 else acc["sq"] + sq
        return run

    def b_finish(r):
        def run():
            rows = slice(r * CHUNK, (r + 1) * CHUNK)
            scale = lax.rsqrt(acc["sq"][rows, :] * (1.0 / D_MODEL) + EPS)
            if final:
                xo_ref[rows, :] = xo_ref[rows, :] * scale * gfin_ref[...]
            else:
                if "gmod" not in acc:
                    acc["gmod"] = gn_ref[layer + 1:layer + 2, :] * (1.0 + sc_ref[pl.ds(b_prev, 1), :])
                    acc["shift"] = sh_ref[pl.ds(b_prev, 1), :]
                hn = (xo_ref[rows, :] * scale * acc["gmod"] + acc["shift"]).astype(bf16)
                hn_ref[rows, :] = hn
                gn_out_ref[rows, :] = lax.dot_general(hn, wg_ref[...].astype(bf16), NT_DIMS,
                                                      preferred_element_type=f32)
        return run

    n_first = 1 + len(stages_a) // cpb
    _interleave(stages_a[:n_first], [b_piece(n) for n in range(D_MODEL // n_piece)])
    _interleave(stages_a[n_first:], [b_finish(r) for r in range(cpb)])

    @pl.when((lax.rem(t, steps_per_b) == steps_per_b - 1) & (t < n_steps))
    def _():
        c_ref[...] = c_s[...]
        n_ref[...] = n_s[...]
        m_ref[...] = m_s[...]


def _mix(p, g, x2, mod, wout, w_s, b_s, g_mh, gfin, b_ig, b_fg, g_norm, w_in_t, prev, *,
         layer, batch, mod_row_block, cpb, final):
    m = x2.shape[0]
    depth = w_s.shape[0]
    ts = cpb * CHUNK
    n_steps = m // ts
    steps_per_b = n_steps // batch
    cur = lambda t: (jnp.minimum(t, n_steps - 1), 0)
    lag = lambda t: (jnp.maximum(t - 1, 0), 0)
    cur_b = lambda t: jnp.minimum(t, n_steps - 1) // steps_per_b
    full2 = lambda t: (0, 0)
    smem = pl.BlockSpec(memory_space=pltpu.SMEM)
    mod_spec = lambda lyr, k: pl.BlockSpec((None, None, 8, D_MODEL), lambda t: (lyr, k, mod_row_block, 0))
    in_specs = [
        smem, smem,
        pl.BlockSpec((ts, N_MAIN), cur),
        pl.BlockSpec((ts, LANES), cur),
        pl.BlockSpec((ts, D_MODEL), lag),
        mod_spec(layer, 2),
        pl.BlockSpec((None, D_MODEL, D_MODEL), lambda t: (layer, 0, 0), pipeline_mode=pl.Buffered(1)),
        pl.BlockSpec((None, GROUPS, CHUNK, CHUNK), lambda t: (layer, 0, 0, 0)),
        pl.BlockSpec((None, GROUPS, CHUNK), lambda t: (layer, 0, 0)),
        pl.BlockSpec(g_mh.shape, full2),
        pl.BlockSpec((1, D_MODEL), full2),
    ]
    args = [b_ig, b_fg, p, g, x2, mod, wout, w_s, b_s, g_mh, gfin]
    out_specs = [pl.BlockSpec((ts, D_MODEL), lag)]
    out_shape = [jax.ShapeDtypeStruct((m, D_MODEL), f32)]
    if not final:
        in_specs += [
            mod_spec(layer + 1, 0), mod_spec(layer + 1, 1),
            pl.BlockSpec(g_norm.shape, full2),
            _gate_rows_spec(layer + 1),
        ]
        args += [mod, mod, g_norm, w_in_t]
        out_specs += [pl.BlockSpec((ts, D_MODEL), lag), pl.BlockSpec((ts, LANES), lag)]
        out_shape += [jax.ShapeDtypeStruct((m, D_MODEL), bf16), jax.ShapeDtypeStruct((m, LANES), f32)]
    n_state_out = len(out_specs)
    out_specs += [
        pl.BlockSpec((None, None, HEADS, HD, HD), lambda t: (layer, cur_b(t), 0, 0, 0)),
        pl.BlockSpec((None, None, HEADS, HD), lambda t: (layer, cur_b(t), 0, 0)),
        pl.BlockSpec((None, None, 8, LANES), lambda t: (layer, cur_b(t), 0, 0)),
    ]
    out_shape += [
        jax.ShapeDtypeStruct((depth, batch, HEADS, HD, HD), f32),
        jax.ShapeDtypeStruct((depth, batch, HEADS, HD), f32),
        jax.ShapeDtypeStruct((depth, batch, 8, LANES), f32),
    ]
    aliases = {}
    if prev is not None:
        for k, arr in enumerate(prev):
            aliases[len(args)] = n_state_out + k
            in_specs.append(pl.BlockSpec(memory_space=pl.ANY))
            args.append(arr)
    kern = functools.partial(_mix_kernel, layer=layer, cpb=cpb, steps_per_b=steps_per_b, n_steps=n_steps,
                             final=final)
    return pl.pallas_call(
        kern,
        grid=(n_steps + 1,),
        in_specs=in_specs,
        out_specs=out_specs,
        out_shape=out_shape,
        scratch_shapes=[
            pltpu.VMEM((2, ts, D_MODEL), bf16),
            pltpu.VMEM((HEADS, HD, HD), f32),
            pltpu.VMEM((HEADS, HD), f32),
            pltpu.VMEM((8, LANES), f32),
        ],
        input_output_aliases=aliases,
        compiler_params=_cparams(("arbitrary",)),
        name="mix",
    )(*args)


def _spre_kernel(bi_ref, bf_ref, p_ref, g_ref, n_ref, m_ref,
                 gk_ref, nn_ref, mn_ref, dec_ref, s_ref, stab_ref, vt_ref, *, layer):
    gt = g_ref[...] + _gate_bias_row(bi_ref, bf_ref, layer)
    lf = _log_sigmoid(gt)
    for h in range(HEADS):
        hs = slice(h * HD, (h + 1) * HD)
        q = p_ref[:, OFF_Q + h * HD:OFF_Q + (h + 1) * HD]
        k = p_ref[:, OFF_K + h * HD:OFF_K + (h + 1) * HD]
        ig = gt[:, h:h + 1]
        a = lf[:, HEADS + h:HEADS + h + 1] + m_ref[:, h:h + 1]
        mt = jnp.maximum(a, ig)
        w_intra = jnp.exp(ig - mt)
        w_inter = jnp.exp(a - mt)
        n_old = n_ref[:, hs]
        s = jnp.sum(q * k, axis=-1, keepdims=True) * w_intra
        den = w_inter * jnp.sum(n_old * q, axis=-1, keepdims=True) + s
        gk = w_intra * k
        gk_ref[:, hs] = gk
        nn_ref[:, hs] = w_inter * n_old + gk
        mn_ref[:, h:h + 1] = mt
        dec_ref[:, h:h + 1] = w_inter
        s_ref[:, h:h + 1] = s
        stab_ref[:, h:h + 1] = jnp.maximum(jnp.abs(den), jnp.exp(-mt))
    for t in range(D_ML // LANES):
        vt_ref[t * LANES:(t + 1) * LANES, :] = p_ref[:, OFF_V + t * LANES:OFF_V + (t + 1) * LANES].T


def _spre(p, g, b_ig, b_fg, state_n, state_m, *, layer):
    b = p.shape[0]
    full = lambda i: (0, 0)
    smem = pl.BlockSpec(memory_space=pltpu.SMEM)
    small = jax.ShapeDtypeStruct((b, HEADS), f32)
    wide = jax.ShapeDtypeStruct((b, D_ML), f32)
    small_spec = pl.BlockSpec((b, HEADS), full)
    wide_spec = pl.BlockSpec((b, D_ML), full)
    return pl.pallas_call(
        functools.partial(_spre_kernel, layer=layer),
        grid=(1,),
        in_specs=[
            smem, smem,
            pl.BlockSpec((b, 3 * D_ML), full),
            pl.BlockSpec((b, LANES), full),
            pl.BlockSpec((None, b, D_ML), lambda i: (layer, 0, 0)),
            pl.BlockSpec((None, b, HEADS), lambda i: (layer, 0, 0)),
        ],
        out_specs=[wide_spec, wide_spec, small_spec, small_spec, small_spec, small_spec,
                   pl.BlockSpec((D_ML, b), full)],
        out_shape=[wide, wide, small, small, small, small, jax.ShapeDtypeStruct((D_ML, b), f32)],
        compiler_params=_cparams(("arbitrary",)),
        name="sample_pre",
    )(b_ig, b_fg, p, g, state_n, state_m)


def _spost_kernel(ws_ref, bs_ref, cqt_ref, p_ref, dec_ref, s_ref, stab_ref, x_ref, gate_ref, wout_ref,
                  gmh_ref, gfin_ref, xo_ref, merged_s, *, layer, final):
    for h in range(HEADS):
        hs = slice(h * HD, (h + 1) * HD)
        cq = jnp.concatenate([cqt_ref[h * HD + t * LANES:h * HD + (t + 1) * LANES, :].T
                              for t in range(HD // LANES)], axis=1)
        v = p_ref[:, OFF_V + h * HD:OFF_V + (h + 1) * HD]
        o_act = p_ref[:, OFF_O + h * HD:OFF_O + (h + 1) * HD]
        z_act = p_ref[:, OFF_ZML + h * HD:OFF_ZML + (h + 1) * HD]
        num = dec_ref[:, h:h + 1] * cq + s_ref[:, h:h + 1] * v
        hml = num / stab_ref[:, h:h + 1]
        y = _rms(o_act * hml, gmh_ref[layer:layer + 1, hs]) * z_act
        merged_s[:, hs] = y.astype(bf16)
    for g in range(GROUPS):
        u_act = p_ref[:, OFF_U + g * HD:OFF_U + (g + 1) * HD]
        vn = p_ref[:, OFF_VN + g * HD:OFF_VN + (g + 1) * HD]
        z_act = p_ref[:, OFF_ZCM + g * HD:OFF_ZCM + (g + 1) * HD]
        sp = ws_ref[layer, g] * vn + bs_ref[layer, g]
        merged_s[:, D_ML + g * HD:D_ML + (g + 1) * HD] = (u_act * sp * z_act).astype(bf16)
    out = jnp.dot(merged_s[...], wout_ref[...], preferred_element_type=f32)
    xn = x_ref[...] + gate_ref[...] * out
    if final:
        xn = _rms(xn, gfin_ref[...])
    xo_ref[...] = xn


def _spost(ws0, bs0, cqt, p, dec, s, stab, x2, mod, wout, g_mh, gfin, *, layer, final):
    b = p.shape[0]
    full = lambda i: (0, 0)
    smem = pl.BlockSpec(memory_space=pltpu.SMEM)
    small = pl.BlockSpec((b, HEADS), full)
    return pl.pallas_call(
        functools.partial(_spost_kernel, layer=layer, final=final),
        grid=(1,),
        in_specs=[
            smem, smem,
            pl.BlockSpec((D_ML, b), full),
            pl.BlockSpec((b, N_MAIN), full),
            small, small, small,
            pl.BlockSpec((b, D_MODEL), full),
            pl.BlockSpec((None, None, b, D_MODEL), lambda i: (layer, 2, MOD_SAMPLE_ROW, 0)),
            pl.BlockSpec((None, D_MODEL, D_MODEL), lambda i: (layer, 0, 0)),
            pl.BlockSpec(g_mh.shape, full),
            pl.BlockSpec((1, D_MODEL), full),
        ],
        out_specs=pl.BlockSpec((b, D_MODEL), full),
        out_shape=jax.ShapeDtypeStruct((b, D_MODEL), f32),
        scratch_shapes=[pltpu.VMEM((b, D_MODEL), bf16)],
        compiler_params=_cparams(("arbitrary",)),
        name="sample_post",
    )(ws0, bs0, cqt, p, dec, s, stab, x2, mod, wout, g_mh, gfin)


def kernel(x_prompt, x_sample, state_C, state_n, state_m, c_prompt, c_sample, g_norm, w_ada, b_ada, w_in,
           b_igate, b_fgate, g_mh, g_cmv, w_s, b_s, w_out, g_final):
    depth = w_in.shape[0]
    bp, seq, _ = x_prompt.shape
    bs = x_sample.shape[0]
    assert bs % 8 == 0 and bp <= 8

    c_all = jnp.concatenate([c_sample, c_prompt, jnp.zeros((8 - bp, D_MODEL), f32)], axis=0)
    mod = _ada(c_all, w_ada, b_ada)
    prompt_row_block = bs // 8
    wout = _cast_bf16(w_out)

    xp = x_prompt.reshape(bp * seq, D_MODEL)
    xs = x_sample.reshape(bs, D_MODEL)
    gfin = g_final.reshape(1, D_MODEL)
    state_n2 = state_n.reshape(depth, bs, D_ML)
    ws0 = w_s[:, :, 0, 0]
    bs0 = b_s[:, :, 0]
    w_in = jnp.swapaxes(w_in, 1, 2)

    hp, gp = _prenorm(xp, mod, g_norm, w_in, layer=0, tm=512, mod_row_block=prompt_row_block, mod_rows=8,
                      per_row_mod=False, rows_per_b=seq)
    prev = None
    cs_all = None
    ns, ms, vs = [], [], []
    for l in range(depth):
        final = l == depth - 1
        hs_, gs_ = _prenorm(xs, mod, g_norm, w_in, layer=l, tm=bs, mod_row_block=MOD_SAMPLE_ROW, mod_rows=bs,
                            per_row_mod=True, rows_per_b=None)
        ps = _in_proj(hs_, w_in, g_cmv, layer=l, tm=bs, out_dtype=f32)
        gk, nn, mn, dec, s, stab, vt = _spre(ps, gs_, b_igate, b_fgate, state_n2, state_m, layer=l)

        p, cs_all, cqt = _in_proj(hp, w_in, g_cmv, layer=l, tm=1024, out_dtype=bf16,
                                  side=(dec.reshape(bs * HEADS), state_C, ps, gk, vt, cs_all))
        xs = _spost(ws0, bs0, cqt, ps, dec, s, stab, xs, mod, wout, g_mh, gfin, layer=l, final=final)
        ns.append(nn.reshape(bs, HEADS, HD))
        ms.append(mn)
        vs.append(ps[:, OFF_VN:OFF_VN + D_CM].reshape(bs, 1, D_CM))

        res = _mix(p, gp, xp, mod, wout, w_s, b_s, g_mh, gfin, b_igate, b_fgate, g_norm, w_in, prev,
                   layer=l, batch=bp, mod_row_block=prompt_row_block, cpb=2, final=final)
        if final:
            xp, *prev = res
        else:
            xp, hp, gp, *prev = res

    cp_all, np_all, mp_all = prev
    return (xp.reshape(bp, seq, D_MODEL), xs.reshape(bs, 1, D_MODEL),
            cp_all, np_all, mp_all[:, :, :HEADS, 0],
            cs_all, jnp.stack(ns), jnp.stack(ms), jnp.stack(vs))
```

```python
import functools

import jax
import jax.numpy as jnp
from jax import lax
from jax.experimental import pallas as pl
from jax.experimental.pallas import tpu as pltpu

f32 = jnp.float32
bf16 = jnp.bfloat16

D_MODEL = 2048
D_ML = 1024
D_CM = 1024
HEADS = 4
HD = 256
GROUPS = 4
CHUNK = 128
LANES = 128
KIND_W = 1024
N_KINDS = 8
N_MAIN = N_KINDS * KIND_W
N_GATES = 2 * HEADS
GATE_KIND = 5
EPS = 1e-6
OFF_Q, OFF_K, OFF_V, OFF_O, OFF_ZML, OFF_U, OFF_VN, OFF_ZCM = (i * KIND_W for i in range(N_KINDS))
MOD_SAMPLE_ROW = 0
VMEM_LIMIT = 56 * 1024 * 1024
NT_DIMS = (((1,), (1,)), ((), ()))


def _cparams(sem):
    return pltpu.CompilerParams(dimension_semantics=sem, vmem_limit_bytes=VMEM_LIMIT)


def _gate_rows_spec(layer):
    blk = GATE_KIND * KIND_W // LANES
    return pl.BlockSpec((None, LANES, D_MODEL), lambda *ids: (layer, blk, 0))


def _sigmoid(x):
    return 0.5 * jnp.tanh(0.5 * x) + 0.5


def _silu(x):
    return x * _sigmoid(x)


def _gelu(x):
    return 0.5 * x * (1.0 + jnp.tanh(0.7978845608028654 * (x + 0.044715 * (x * x * x))))


def _log_sigmoid(x):
    return jnp.minimum(x, 0.0) - jnp.log1p(jnp.exp(-jnp.abs(x)))


def _rms(x, g):
    return x * lax.rsqrt(jnp.mean(x * x, axis=-1, keepdims=True) + EPS) * g


def _layernorm(x, g):
    mu = jnp.mean(x, axis=-1, keepdims=True)
    xc = x - mu
    return xc * lax.rsqrt(jnp.mean(xc * xc, axis=-1, keepdims=True) + EPS) * g


def _gate_bias_row(bi_ref, bf_ref, layer):
    lane = lax.broadcasted_iota(jnp.int32, (1, LANES), 1)
    row = jnp.zeros((1, LANES), f32)
    for h in range(HEADS):
        row = jnp.where(lane == h, bi_ref[layer, h], row)
        row = jnp.where(lane == HEADS + h, bf_ref[layer, h], row)
    return row


def _modulated_norm(x, gn, sc, sh):
    return _rms(x, gn) * (1.0 + sc) + sh


def _interleave(first, second):
    done = 0
    for k, stage in enumerate(first):
        stage()
        want = ((k + 1) * len(second)) // len(first)
        while done < want:
            second[done]()
            done += 1


def _ada_kernel(c_ref, w_ref, b_ref, o_ref):
    a = _silu(c_ref[...]).astype(bf16)
    o_ref[...] = jnp.dot(a, w_ref[...].astype(bf16), preferred_element_type=f32) + b_ref[...]


def _ada(c_all, w_ada, b_ada):
    depth, _, n3 = w_ada.shape
    rows = c_all.shape[0]
    tn = 512
    per = D_MODEL // tn
    return pl.pallas_call(
        _ada_kernel,
        grid=(depth, n3 // tn),
        in_specs=[
            pl.BlockSpec((rows, D_MODEL), lambda l, j: (0, 0)),
            pl.BlockSpec((None, D_MODEL, tn), lambda l, j: (l, 0, j)),
            pl.BlockSpec((None, 1, tn), lambda l, j: (l, 0, j)),
        ],
        out_specs=pl.BlockSpec((None, None, rows, tn), lambda l, j: (l, j // per, 0, j % per)),
        out_shape=jax.ShapeDtypeStruct((depth, 3, rows, D_MODEL), f32),
        compiler_params=_cparams(("arbitrary", "arbitrary")),
        name="ada",
    )(c_all, w_ada, b_ada.reshape(depth, 1, n3))


def _cast_kernel(w_ref, o_ref):
    o_ref[...] = w_ref[...].astype(bf16)


def _cast_bf16(w):
    depth, r, c = w.shape
    tr = 512
    return pl.pallas_call(
        _cast_kernel,
        grid=(depth, r // tr),
        in_specs=[pl.BlockSpec((None, tr, c), lambda l, i: (l, i, 0))],
        out_specs=pl.BlockSpec((None, tr, c), lambda l, i: (l, i, 0)),
        out_shape=jax.ShapeDtypeStruct(w.shape, bf16),
        compiler_params=_cparams(("arbitrary", "arbitrary")),
        name="cast_w_out",
    )(w)


def _prenorm_kernel(x_ref, sh_ref, sc_ref, gn_ref, wg_ref, h_ref, g_ref, *, layer, per_row_mod, rows_per_b):
    if per_row_mod:
        sh, sc = sh_ref[...], sc_ref[...]
    else:
        b = (pl.program_id(0) * x_ref.shape[0]) // rows_per_b
        sh, sc = sh_ref[pl.ds(b, 1), :], sc_ref[pl.ds(b, 1), :]
    h = _modulated_norm(x_ref[...], gn_ref[layer:layer + 1, :], sc, sh).astype(bf16)
    h_ref[...] = h
    g_ref[...] = lax.dot_general(h, wg_ref[...].astype(bf16), NT_DIMS, preferred_element_type=f32)


def _prenorm(x2, mod, g_norm, w_in_t, *, layer, tm, mod_row_block, mod_rows, per_row_mod, rows_per_b):
    m = x2.shape[0]
    mod_spec = lambda k: pl.BlockSpec((None, None, mod_rows, D_MODEL), lambda i: (layer, k, mod_row_block, 0))
    return pl.pallas_call(
        functools.partial(_prenorm_kernel, layer=layer, per_row_mod=per_row_mod, rows_per_b=rows_per_b),
        grid=(m // tm,),
        in_specs=[
            pl.BlockSpec((tm, D_MODEL), lambda i: (i, 0)),
            mod_spec(0),
            mod_spec(1),
            pl.BlockSpec(g_norm.shape, lambda i: (0, 0)),
            _gate_rows_spec(layer),
        ],
        out_specs=[pl.BlockSpec((tm, D_MODEL), lambda i: (i, 0)), pl.BlockSpec((tm, LANES), lambda i: (i, 0))],
        out_shape=[jax.ShapeDtypeStruct((m, D_MODEL), bf16), jax.ShapeDtypeStruct((m, LANES), f32)],
        compiler_params=_cparams(("arbitrary",)),
        name="prenorm",
    )(x2, mod, mod, g_norm, w_in_t)


def _in_kernel(h_ref, wm_ref, wx_ref, gcmv_ref, *rest, layer, tm, rc, n_i, side_bb):
    if side_bb:
        dec_ref, c_ref, q_ref, gk_ref, vt_ref = rest[:5]
        p_ref, co_ref, cqt_ref, wb_s = rest[-4:]
    else:
        p_ref, wb_s = rest[-2:]
    j = pl.program_id(0)
    i = pl.program_id(1)
    kc = 256
    nk = KIND_W // kc

    @pl.when(i == 0)
    def _():
        @pl.when(j < GATE_KIND)
        def _():
            for r in range(nk):
                wb_s[r * kc:(r + 1) * kc, :] = wm_ref[r * kc:(r + 1) * kc, :].astype(bf16)

        @pl.when(j >= GATE_KIND)
        def _():
            for r in range(nk - 1):
                wb_s[r * kc:(r + 1) * kc, :] = wm_ref[r * kc + N_GATES:(r + 1) * kc + N_GATES, :].astype(bf16)
            last = jnp.concatenate([wm_ref[(nk - 1) * kc + N_GATES:, :], wx_ref[...]], axis=0)
            wb_s[(nk - 1) * kc:, :] = last.astype(bf16)

    if side_bb:
        step = j * n_i + i

        @pl.when(step == 0)
        def _():
            cqt_ref[...] = jnp.zeros_like(cqt_ref)

        lane = lax.broadcasted_iota(jnp.int32, (HD, cqt_ref.shape[1]), 1)

        def side_unit(u):
            ib, h = divmod(u, HEADS)
            b = step * side_bb + ib
            hs = slice(h * HD, (h + 1) * HD)
            onehot = (lane == b).astype(f32)
            c_old = c_ref[ib, h]
            vcol = jnp.sum(vt_ref[hs, :] * onehot, axis=-1, keepdims=True)
            cq = jnp.sum(c_old * q_ref[pl.ds(b, 1), hs], axis=-1, keepdims=True)
            cqt_ref[hs, :] += cq * onehot
            co_ref[ib, h] = dec_ref[b * HEADS + h] * c_old + vcol * gk_ref[pl.ds(b, 1), hs]

        side = [functools.partial(side_unit, u) for u in range(side_bb * HEADS)]
    else:
        side = []

    def vn_act(acc):
        g = gcmv_ref[layer:layer + 1, :]
        return jnp.concatenate([_layernorm(_gelu(acc[:, t * HD:(t + 1) * HD]), g[:, t * HD:(t + 1) * HD])
                                for t in range(GROUPS)], axis=1)

    def run(act):
        def chunk(r):
            acc = lax.dot_general(h_ref[r * rc:(r + 1) * rc, :], wb_s[...], NT_DIMS, preferred_element_type=f32)
            p_ref[r * rc:(r + 1) * rc, :] = act(acc).astype(p_ref.dtype)

        _interleave([functools.partial(chunk, r) for r in range(tm // rc)], side)

    pl.when((j == 0) | (j == 2))(lambda: run(lambda a: a))
    pl.when(j == 1)(lambda: run(lambda a: a * (HD ** -0.5)))
    pl.when(j == 3)(lambda: run(_sigmoid))
    pl.when((j == 4) | (j == 7))(lambda: run(_silu))
    pl.when(j == 5)(lambda: run(_gelu))
    pl.when(j == 6)(lambda: run(vn_act))


def _in_proj(h, w_in_t, g_cmv, *, layer, tm, out_dtype, side=None):
    m = h.shape[0]
    rc = min(tm, 256)
    n_i = m // tm
    in_specs = [
        pl.BlockSpec((tm, D_MODEL), lambda j, i: (i, 0)),
        pl.BlockSpec((None, KIND_W, D_MODEL), lambda j, i: (layer, j, 0)),
        pl.BlockSpec((None, N_GATES, D_MODEL), lambda j, i: (layer, (j + 1) * (KIND_W // N_GATES), 0)),
        pl.BlockSpec(g_cmv.shape, lambda j, i: (0, 0)),
    ]
    args = [h, w_in_t, w_in_t, g_cmv]
    out_specs = [pl.BlockSpec((tm, KIND_W), lambda j, i: (i, j))]
    out_shape = [jax.ShapeDtypeStruct((m, N_MAIN), out_dtype)]
    aliases = {}
    side_bb = 0
    if side is not None:
        dec_flat, state_c, p_s, gk, vt, c_out_prev = side
        bs = p_s.shape[0]
        side_bb = bs // (N_KINDS * n_i)
        assert side_bb * N_KINDS * n_i == bs
        blk = lambda j, i: (layer, j * n_i + i, 0, 0, 0)
        in_specs += [
            pl.BlockSpec(memory_space=pltpu.SMEM),
            pl.BlockSpec((None, side_bb, HEADS, HD, HD), blk),
            pl.BlockSpec((bs, D_ML), lambda j, i: (0, 0)),
            pl.BlockSpec((bs, D_ML), lambda j, i: (0, 0)),
            pl.BlockSpec((D_ML, bs), lambda j, i: (0, 0)),
        ]
        args += [dec_flat, state_c, p_s, gk, vt]
        out_specs += [pl.BlockSpec((None, side_bb, HEADS, HD, HD), blk),
                      pl.BlockSpec((D_ML, bs), lambda j, i: (0, 0))]
        out_shape += [jax.ShapeDtypeStruct(state_c.shape, f32), jax.ShapeDtypeStruct((D_ML, bs), f32)]
        if c_out_prev is not None:
            aliases[len(args)] = 1
            in_specs.append(pl.BlockSpec(memory_space=pl.ANY))
            args.append(c_out_prev)
    res = pl.pallas_call(
        functools.partial(_in_kernel, layer=layer, tm=tm, rc=rc, n_i=n_i, side_bb=side_bb),
        grid=(N_KINDS, n_i),
        in_specs=in_specs,
        out_specs=out_specs,
        out_shape=out_shape,
        scratch_shapes=[pltpu.VMEM((KIND_W, D_MODEL), bf16)],
        input_output_aliases=aliases,
        compiler_params=_cparams(("arbitrary", "arbitrary")),
        name="in_proj",
    )(*args)
    return res if side is not None else res[0]


def _mix_kernel(bi_ref, bf_ref, p_ref, g_ref, x_ref, gate_ref, wout_ref, ws_ref, bs_ref, gmh_ref, gfin_ref,
                *rest, layer, cpb, steps_per_b, n_steps, final):
    if final:
        xo_ref, c_ref, n_ref, m_ref = rest[-8:-4]
    else:
        sh_ref, sc_ref, gn_ref, wg_ref = rest[:4]
        xo_ref, hn_ref, gn_out_ref, c_ref, n_ref, m_ref = rest[-10:-4]
    merged_s, c_s, n_s, m_s = rest[-4:]
    t = pl.program_id(0)
    ts = cpb * CHUNK
    slot_a = lax.rem(t, 2)
    slot_b = 1 - slot_a
    b_prev = jnp.maximum(t - 1, 0) // steps_per_b

    @pl.when(t == 0)
    def _():
        merged_s[1] = jnp.zeros((ts, D_MODEL), bf16)

    @pl.when(lax.rem(t, steps_per_b) == 0)
    def _():
        c_s[...] = jnp.zeros_like(c_s)
        n_s[...] = jnp.zeros_like(n_s)
        m_s[...] = jnp.zeros_like(m_s)

    row = lax.broadcasted_iota(jnp.int32, (CHUNK, CHUNK), 0)
    col = lax.broadcasted_iota(jnp.int32, (CHUNK, CHUNK), 1)
    causal = col <= row
    trilb = causal.astype(bf16)
    bias_row = _gate_bias_row(bi_ref, bf_ref, layer)
    nt = NT_DIMS
    hd_cols = lambda off, i: slice(off + i * HD, off + (i + 1) * HD)
    heads = range(HEADS)
    groups = range(GROUPS)
    env = {}

    def a_prologue():
        env["ws"] = [jnp.where(causal, ws_ref[g], 0.0).astype(bf16) for g in groups]
        env["bs"] = [jnp.sum(jnp.where(row == col, bs_ref[g:g + 1, :], 0.0), axis=-1, keepdims=True)
                     for g in groups]
        gts = [g_ref[ci * CHUNK:(ci + 1) * CHUNK, :] + bias_row for ci in range(cpb)]
        lf = jnp.concatenate([_log_sigmoid(gt) for gt in gts], axis=1)
        hi = lf.astype(bf16)
        r1 = lf - hi.astype(f32)
        mid = r1.astype(bf16)
        lo = (r1 - mid.astype(f32)).astype(bf16)
        cum_cat = (jnp.dot(trilb, hi, preferred_element_type=f32) + jnp.dot(trilb, mid, preferred_element_type=f32)
                   + jnp.dot(trilb, lo, preferred_element_type=f32))
        cums = [cum_cat[:, ci * LANES:(ci + 1) * LANES] for ci in range(cpb)]
        env["gt"], env["cum"] = gts, cums
        env["gt_t"] = [gt.T for gt in gts]
        env["cum_t"] = [cum.T for cum in cums]

    def a_chunk(ci):
        rows = slice(ci * CHUNK, (ci + 1) * CHUNK)
        e = {}

        def s1():
            gt, gt_t, cum, cum_t = env["gt"][ci], env["gt_t"][ci], env["cum"][ci], env["cum_t"][ci]
            e["qb"] = [p_ref[rows, hd_cols(OFF_Q, h)] for h in heads]
            e["kb"] = [p_ref[rows, hd_cols(OFF_K, h)] for h in heads]
            e["ig_col"] = [gt[:, h:h + 1] for h in heads]
            ig_row = e["ig_row"] = [gt_t[h:h + 1, :] for h in heads]
            e["b_col"] = [cum[:, HEADS + h:HEADS + h + 1] for h in heads]
            b_row = e["b_row"] = [cum_t[HEADS + h:HEADS + h + 1, :] for h in heads]
            e["m_old"] = [m_s[h:h + 1, 0:1] for h in heads]
            e["n_old"] = [n_s[h:h + 1, :] for h in heads]
            e["dmat"] = [jnp.where(causal, e["b_col"][h] - b_row[h] + ig_row[h], -jnp.inf) for h in heads]
            e["a_col"] = [e["b_col"][h] + e["m_old"][h] for h in heads]
            e["dmax"] = [jnp.max(e["dmat"][h], axis=-1, keepdims=True) for h in heads]

        def s2():
            e["qk"] = [lax.dot_general(e["qb"][h], e["kb"][h], nt, preferred_element_type=f32) for h in heads]
            e["cq"] = [lax.dot_general(e["qb"][h], c_s[h].astype(bf16), nt, preferred_element_type=f32)
                       for h in heads]
            n_rows = [jnp.broadcast_to(e["n_old"][h].astype(bf16), (16, HD)) for h in heads]
            e["qn"] = [lax.dot_general(e["qb"][h], n_rows[h], nt, preferred_element_type=f32)[:, 0:1]
                       for h in heads]

        def s3():
            e["mt"] = [jnp.maximum(e["a_col"][h], e["dmax"][h]) for h in heads]
            w_intra = [jnp.exp(e["dmat"][h] - e["mt"][h]) for h in heads]
            e["w_inter"] = [jnp.exp(e["a_col"][h] - e["mt"][h]) for h in heads]
            e["s"] = [e["qk"][h] * w_intra[h] for h in heads]
            e["vb"] = [p_ref[rows, hd_cols(OFF_V, h)] for h in heads]

        def s4():
            e["sv"] = [jnp.dot(e["s"][h].astype(bf16), e["vb"][h], preferred_element_type=f32) for h in heads]
            e["ssum"] = [jnp.sum(e["s"][h], axis=-1, keepdims=True) for h in heads]

        def s5():
            e["m_new"] = [e["mt"][h][CHUNK - 1:CHUNK, :] for h in heads]
            b_last = [e["b_col"][h][CHUNK - 1:CHUNK, :] for h in heads]
            e["g_end"] = [jnp.exp(b_last[h] - e["b_col"][h] + e["ig_col"][h] - e["m_new"][h]) for h in heads]
            e["decay"] = [jnp.exp(b_last[h] + e["m_old"][h] - e["m_new"][h]) for h in heads]
            g_rows = [jnp.broadcast_to(jnp.exp(b_last[h] - e["b_row"][h] + e["ig_row"][h] - e["m_new"][h]),
                                       (16, CHUNK)) for h in heads]
            e["gv_t"] = [jnp.concatenate([(e["g_end"][h] * e["vb"][h].astype(f32)).T, g_rows[h]],
                                         axis=0).astype(bf16) for h in heads]

        def s6():
            kvg = [jnp.dot(e["gv_t"][h], e["kb"][h], preferred_element_type=f32) for h in heads]
            for h in heads:
                c_s[h] = e["decay"][h] * c_s[h] + kvg[h][:HD, :]
                n_s[h:h + 1, :] = e["decay"][h] * e["n_old"][h] + kvg[h][HD:HD + 1, :]
                m_s[h:h + 1, :] = jnp.broadcast_to(e["m_new"][h], (1, m_s.shape[1]))

        def s7():
            den = [e["w_inter"][h] * e["qn"][h] + e["ssum"][h] for h in heads]
            rden = [1.0 / jnp.maximum(jnp.abs(den[h]), jnp.exp(-e["mt"][h])) for h in heads]
            hml = [(e["w_inter"][h] * e["cq"][h] + e["sv"][h]) * rden[h] for h in heads]
            e["og"] = [p_ref[rows, hd_cols(OFF_O, h)].astype(f32) * hml[h] for h in heads]
            e["ms"] = [jnp.mean(e["og"][h] * e["og"][h], axis=-1, keepdims=True) for h in heads]

        def s8():
            for h in heads:
                y = (e["og"][h] * lax.rsqrt(e["ms"][h] + EPS) * gmh_ref[layer:layer + 1, hd_cols(0, h)]
                     * p_ref[rows, hd_cols(OFF_ZML, h)].astype(f32))
                merged_s[slot_a, rows, hd_cols(0, h)] = y.astype(bf16)

        def s9():
            e["sp"] = [jnp.dot(env["ws"][g], p_ref[rows, hd_cols(OFF_VN, g)], preferred_element_type=f32)
                       for g in groups]

        def s10():
            for g in groups:
                y = (p_ref[rows, hd_cols(OFF_U, g)].astype(f32) * (e["sp"][g] + env["bs"][g])
                     * p_ref[rows, hd_cols(OFF_ZCM, g)].astype(f32))
                merged_s[slot_a, rows, hd_cols(D_ML, g)] = y.astype(bf16)

        return [s1, s2, s3, s4, s5, s6, s7, s8, s9, s10]

    stages_a = [a_prologue]
    for ci in range(cpb):
        stages_a += a_chunk(ci)

    n_piece = 1024
    gate_row = gate_ref[pl.ds(b_prev, 1), :]
    acc = {}

    def b_piece(n):
        def run():
            cols = slice(n * n_piece, (n + 1) * n_piece)
            out = jnp.dot(merged_s[slot_b], wout_ref[:, cols], preferred_element_type=f32)
            xn = x_ref[:, cols] + gate_row[:, cols] * out
            xo_ref[:, cols] = xn
            sq = jnp.sum(xn * xn, axis=-1, keepdims=True)
            acc["sq"] = sq if n == 0 else acc["sq"] + sq
        return run

    def b_finish(r):
        def run():
            rows = slice(r * CHUNK, (r + 1) * CHUNK)
            scale = lax.rsqrt(acc["sq"][rows, :] * (1.0 / D_MODEL) + EPS)
            if final:
                xo_ref[rows, :] = xo_ref[rows, :] * scale * gfin_ref[...]
            else:
                if "gmod" not in acc:
                    acc["gmod"] = gn_ref[layer + 1:layer + 2, :] * (1.0 + sc_ref[pl.ds(b_prev, 1), :])
                    acc["shift"] = sh_ref[pl.ds(b_prev, 1), :]
                hn = (xo_ref[rows, :] * scale * acc["gmod"] + acc["shift"]).astype(bf16)
                hn_ref[rows, :] = hn
                gn_out_ref[rows, :] = lax.dot_general(hn, wg_ref[...].astype(bf16), NT_DIMS,
                                                      preferred_element_type=f32)
        return run

    n_first = 1 + len(stages_a) // cpb
    n_lead = 4
    for stage in stages_a[:n_lead]:
        stage()
    b_piece(0)()
    for stage in stages_a[n_lead:n_lead + 4]:
        stage()
    b_piece(1)()
    for stage in stages_a[n_lead + 4:n_first]:
        stage()
    _interleave(stages_a[n_first:], [b_finish(r) for r in range(cpb)])

    @pl.when((lax.rem(t, steps_per_b) == steps_per_b - 1) & (t < n_steps))
    def _():
        c_ref[...] = c_s[...]
        n_ref[...] = n_s[...]
        m_ref[...] = m_s[...]


def _mix(p, g, x2, mod, wout, w_s, b_s, g_mh, gfin, b_ig, b_fg, g_norm, w_in_t, prev, *,
         layer, batch, mod_row_block, cpb, final):
    m = x2.shape[0]
    depth = w_s.shape[0]
    ts = cpb * CHUNK
    n_steps = m // ts
    steps_per_b = n_steps // batch
    cur = lambda t: (jnp.minimum(t, n_steps - 1), 0)
    lag = lambda t: (jnp.maximum(t - 1, 0), 0)
    cur_b = lambda t: jnp.minimum(t, n_steps - 1) // steps_per_b
    full2 = lambda t: (0, 0)
    smem = pl.BlockSpec(memory_space=pltpu.SMEM)
    mod_spec = lambda lyr, k: pl.BlockSpec((None, None, 8, D_MODEL), lambda t: (lyr, k, mod_row_block, 0))
    in_specs = [
        smem, smem,
        pl.BlockSpec((ts, N_MAIN), cur),
        pl.BlockSpec((ts, LANES), cur),
        pl.BlockSpec((ts, D_MODEL), lag),
        mod_spec(layer, 2),
        pl.BlockSpec((None, D_MODEL, D_MODEL), lambda t: (layer, 0, 0), pipeline_mode=pl.Buffered(1)),
        pl.BlockSpec((None, GROUPS, CHUNK, CHUNK), lambda t: (layer, 0, 0, 0)),
        pl.BlockSpec((None, GROUPS, CHUNK), lambda t: (layer, 0, 0)),
        pl.BlockSpec(g_mh.shape, full2),
        pl.BlockSpec((1, D_MODEL), full2),
    ]
    args = [b_ig, b_fg, p, g, x2, mod, wout, w_s, b_s, g_mh, gfin]
    out_specs = [pl.BlockSpec((ts, D_MODEL), lag)]
    out_shape = [jax.ShapeDtypeStruct((m, D_MODEL), f32)]
    if not final:
        in_specs += [
            mod_spec(layer + 1, 0), mod_spec(layer + 1, 1),
            pl.BlockSpec(g_norm.shape, full2),
            _gate_rows_spec(layer + 1),
        ]
        args += [mod, mod, g_norm, w_in_t]
        out_specs += [pl.BlockSpec((ts, D_MODEL), lag), pl.BlockSpec((ts, LANES), lag)]
        out_shape += [jax.ShapeDtypeStruct((m, D_MODEL), bf16), jax.ShapeDtypeStruct((m, LANES), f32)]
    n_state_out = len(out_specs)
    out_specs += [
        pl.BlockSpec((None, None, HEADS, HD, HD), lambda t: (layer, cur_b(t), 0, 0, 0)),
        pl.BlockSpec((None, None, HEADS, HD), lambda t: (layer, cur_b(t), 0, 0)),
        pl.BlockSpec((None, None, 8, LANES), lambda t: (layer, cur_b(t), 0, 0)),
    ]
    out_shape += [
        jax.ShapeDtypeStruct((depth, batch, HEADS, HD, HD), f32),
        jax.ShapeDtypeStruct((depth, batch, HEADS, HD), f32),
        jax.ShapeDtypeStruct((depth, batch, 8, LANES), f32),
    ]
    aliases = {}
    if prev is not None:
        for k, arr in enumerate(prev):
            aliases[len(args)] = n_state_out + k
            in_specs.append(pl.BlockSpec(memory_space=pl.ANY))
            args.append(arr)
    kern = functools.partial(_mix_kernel, layer=layer, cpb=cpb, steps_per_b=steps_per_b, n_steps=n_steps,
                             final=final)
    return pl.pallas_call(
        kern,
        grid=(n_steps + 1,),
        in_specs=in_specs,
        out_specs=out_specs,
        out_shape=out_shape,
        scratch_shapes=[
            pltpu.VMEM((2, ts, D_MODEL), bf16),
            pltpu.VMEM((HEADS, HD, HD), f32),
            pltpu.VMEM((HEADS, HD), f32),
            pltpu.VMEM((8, LANES), f32),
        ],
        input_output_aliases=aliases,
        compiler_params=_cparams(("arbitrary",)),
        name="mix",
    )(*args)


def _spre_kernel(bi_ref, bf_ref, p_ref, g_ref, n_ref, m_ref,
                 gk_ref, nn_ref, mn_ref, dec_ref, s_ref, stab_ref, vt_ref, *, layer):
    gt = g_ref[...] + _gate_bias_row(bi_ref, bf_ref, layer)
    lf = _log_sigmoid(gt)
    for h in range(HEADS):
        hs = slice(h * HD, (h + 1) * HD)
        q = p_ref[:, OFF_Q + h * HD:OFF_Q + (h + 1) * HD]
        k = p_ref[:, OFF_K + h * HD:OFF_K + (h + 1) * HD]
        ig = gt[:, h:h + 1]
        a = lf[:, HEADS + h:HEADS + h + 1] + m_ref[:, h:h + 1]
        mt = jnp.maximum(a, ig)
        w_intra = jnp.exp(ig - mt)
        w_inter = jnp.exp(a - mt)
        n_old = n_ref[:, hs]
        s = jnp.sum(q * k, axis=-1, keepdims=True) * w_intra
        den = w_inter * jnp.sum(n_old * q, axis=-1, keepdims=True) + s
        gk = w_intra * k
        gk_ref[:, hs] = gk
        nn_ref[:, hs] = w_inter * n_old + gk
        mn_ref[:, h:h + 1] = mt
        dec_ref[:, h:h + 1] = w_inter
        s_ref[:, h:h + 1] = s
        stab_ref[:, h:h + 1] = jnp.maximum(jnp.abs(den), jnp.exp(-mt))
    for t in range(D_ML // LANES):
        vt_ref[t * LANES:(t + 1) * LANES, :] = p_ref[:, OFF_V + t * LANES:OFF_V + (t + 1) * LANES].T


def _spre(p, g, b_ig, b_fg, state_n, state_m, *, layer):
    b = p.shape[0]
    full = lambda i: (0, 0)
    smem = pl.BlockSpec(memory_space=pltpu.SMEM)
    small = jax.ShapeDtypeStruct((b, HEADS), f32)
    wide = jax.ShapeDtypeStruct((b, D_ML), f32)
    small_spec = pl.BlockSpec((b, HEADS), full)
    wide_spec = pl.BlockSpec((b, D_ML), full)
    return pl.pallas_call(
        functools.partial(_spre_kernel, layer=layer),
        grid=(1,),
        in_specs=[
            smem, smem,
            pl.BlockSpec((b, 3 * D_ML), full),
            pl.BlockSpec((b, LANES), full),
            pl.BlockSpec((None, b, D_ML), lambda i: (layer, 0, 0)),
            pl.BlockSpec((None, b, HEADS), lambda i: (layer, 0, 0)),
        ],
        out_specs=[wide_spec, wide_spec, small_spec, small_spec, small_spec, small_spec,
                   pl.BlockSpec((D_ML, b), full)],
        out_shape=[wide, wide, small, small, small, small, jax.ShapeDtypeStruct((D_ML, b), f32)],
        compiler_params=_cparams(("arbitrary",)),
        name="sample_pre",
    )(b_ig, b_fg, p, g, state_n, state_m)


def _spost_kernel(ws_ref, bs_ref, cqt_ref, p_ref, dec_ref, s_ref, stab_ref, x_ref, gate_ref, wout_ref,
                  gmh_ref, gfin_ref, xo_ref, merged_s, *, layer, final):
    for h in range(HEADS):
        hs = slice(h * HD, (h + 1) * HD)
        cq = jnp.concatenate([cqt_ref[h * HD + t * LANES:h * HD + (t + 1) * LANES, :].T
                              for t in range(HD // LANES)], axis=1)
        v = p_ref[:, OFF_V + h * HD:OFF_V + (h + 1) * HD]
        o_act = p_ref[:, OFF_O + h * HD:OFF_O + (h + 1) * HD]
        z_act = p_ref[:, OFF_ZML + h * HD:OFF_ZML + (h + 1) * HD]
        num = dec_ref[:, h:h + 1] * cq + s_ref[:, h:h + 1] * v
        hml = num / stab_ref[:, h:h + 1]
        y = _rms(o_act * hml, gmh_ref[layer:layer + 1, hs]) * z_act
        merged_s[:, hs] = y.astype(bf16)
    for g in range(GROUPS):
        u_act = p_ref[:, OFF_U + g * HD:OFF_U + (g + 1) * HD]
        vn = p_ref[:, OFF_VN + g * HD:OFF_VN + (g + 1) * HD]
        z_act = p_ref[:, OFF_ZCM + g * HD:OFF_ZCM + (g + 1) * HD]
        sp = ws_ref[layer, g] * vn + bs_ref[layer, g]
        merged_s[:, D_ML + g * HD:D_ML + (g + 1) * HD] = (u_act * sp * z_act).astype(bf16)
    out = jnp.dot(merged_s[...], wout_ref[...], preferred_element_type=f32)
    xn = x_ref[...] + gate_ref[...] * out
    if final:
        xn = _rms(xn, gfin_ref[...])
    xo_ref[...] = xn


def _spost(ws0, bs0, cqt, p, dec, s, stab, x2, mod, wout, g_mh, gfin, *, layer, final):
    b = p.shape[0]
    full = lambda i: (0, 0)
    smem = pl.BlockSpec(memory_space=pltpu.SMEM)
    small = pl.BlockSpec((b, HEADS), full)
    return pl.pallas_call(
        functools.partial(_spost_kernel, layer=layer, final=final),
        grid=(1,),
        in_specs=[
            smem, smem,
            pl.BlockSpec((D_ML, b), full),
            pl.BlockSpec((b, N_MAIN), full),
            small, small, small,
            pl.BlockSpec((b, D_MODEL), full),
            pl.BlockSpec((None, None, b, D_MODEL), lambda i: (layer, 2, MOD_SAMPLE_ROW, 0)),
            pl.BlockSpec((None, D_MODEL, D_MODEL), lambda i: (layer, 0, 0)),
            pl.BlockSpec(g_mh.shape, full),
            pl.BlockSpec((1, D_MODEL), full),
        ],
        out_specs=pl.BlockSpec((b, D_MODEL), full),
        out_shape=jax.ShapeDtypeStruct((b, D_MODEL), f32),
        scratch_shapes=[pltpu.VMEM((b, D_MODEL), bf16)],
        compiler_params=_cparams(("arbitrary",)),
        name="sample_post",
    )(ws0, bs0, cqt, p, dec, s, stab, x2, mod, wout, g_mh, gfin)


def kernel(x_prompt, x_sample, state_C, state_n, state_m, c_prompt, c_sample, g_norm, w_ada, b_ada, w_in,
           b_igate, b_fgate, g_mh, g_cmv, w_s, b_s, w_out, g_final):
    depth = w_in.shape[0]
    bp, seq, _ = x_prompt.shape
    bs = x_sample.shape[0]
    assert bs % 8 == 0 and bp <= 8

    c_all = jnp.concatenate([c_sample, c_prompt, jnp.zeros((8 - bp, D_MODEL), f32)], axis=0)
    mod = _ada(c_all, w_ada, b_ada)
    prompt_row_block = bs // 8
    wout = _cast_bf16(w_out)

    xp = x_prompt.reshape(bp * seq, D_MODEL)
    xs = x_sample.reshape(bs, D_MODEL)
    gfin = g_final.reshape(1, D_MODEL)
    state_n2 = state_n.reshape(depth, bs, D_ML)
    ws0 = w_s[:, :, 0, 0]
    bs0 = b_s[:, :, 0]
    w_in = jnp.swapaxes(w_in, 1, 2)

    hp, gp = _prenorm(xp, mod, g_norm, w_in, layer=0, tm=512, mod_row_block=prompt_row_block, mod_rows=8,
                      per_row_mod=False, rows_per_b=seq)
    prev = None
    cs_all = None
    ns, ms, vs = [], [], []
    for l in range(depth):
        final = l == depth - 1
        hs_, gs_ = _prenorm(xs, mod, g_norm, w_in, layer=l, tm=bs, mod_row_block=MOD_SAMPLE_ROW, mod_rows=bs,
                            per_row_mod=True, rows_per_b=None)
        ps = _in_proj(hs_, w_in, g_cmv, layer=l, tm=bs, out_dtype=f32)
        gk, nn, mn, dec, s, stab, vt = _spre(ps, gs_, b_igate, b_fgate, state_n2, state_m, layer=l)

        p, cs_all, cqt = _in_proj(hp, w_in, g_cmv, layer=l, tm=1024, out_dtype=bf16,
                                  side=(dec.reshape(bs * HEADS), state_C, ps, gk, vt, cs_all))
        xs = _spost(ws0, bs0, cqt, ps, dec, s, stab, xs, mod, wout, g_mh, gfin, layer=l, final=final)
        ns.append(nn.reshape(bs, HEADS, HD))
        ms.append(mn)
        vs.append(ps[:, OFF_VN:OFF_VN + D_CM].reshape(bs, 1, D_CM))

        res = _mix(p, gp, xp, mod, wout, w_s, b_s, g_mh, gfin, b_igate, b_fgate, g_norm, w_in, prev,
                   layer=l, batch=bp, mod_row_block=prompt_row_block, cpb=2, final=final)
        if final:
            xp, *prev = res
        else:
            xp, hp, gp, *prev = res

    cp_all, np_all, mp_all = prev
    return (xp.reshape(bp, seq, D_MODEL), xs.reshape(bs, 1, D_MODEL),
            cp_all, np_all, mp_all[:, :, :HEADS, 0],
            cs_all, jnp.stack(ns), jnp.stack(ms), jnp.stack(vs))
```
